```python
import jax, jax.numpy as jnp
from jax import lax
import numpy as np

D_MODEL = 2048
BATCH = 8
SEQ = 2048
DEPTH = 1

CHUNK = 64
HEAD_DIM = 128
N_HEADS_SB = 8
N_HEADS_CA = 8
W_SB = N_HEADS_SB * HEAD_DIM
W_CA = N_HEADS_CA * HEAD_DIM
LEFT_CHUNKS = 8
BAND = (LEFT_CHUNKS + 1) * CHUNK
REL_CLIP = 128
N_REL = REL_CLIP + CHUNK
Q_BLOCK = 128
D_FF = -(-8 * D_MODEL // (3 * 256)) * 256
D_PLE = 256
EPS = 1e-6
NEG = -1e30
IN_COLS = 3 * W_SB + 3 * W_CA + 2 * D_MODEL

kernel_name = "hybrid_stickbreak_chunkrel_block"


def rmsnorm(x, g):
    xf = x.astype(jnp.float32)
    y = xf * lax.rsqrt(jnp.mean(xf * xf, axis=-1, keepdims=True) + EPS)
    return (y * g.astype(jnp.float32)).astype(x.dtype)


def stick_breaking_attention(q, k, v):
    B, S, H, Dh = q.shape
    scale = Dh ** -0.5
    outs = []
    for qb in range(S // Q_BLOCK):
        t0 = qb * Q_BLOCK
        t1 = t0 + Q_BLOCK
        kb = k[:, :t1]
        vb = v[:, :t1]
        z = jnp.einsum('bqhd,bkhd->bhqk', q[:, t0:t1], kb).astype(jnp.float32) * scale
        past = jnp.arange(t1)[None, :] < jnp.arange(t0, t1)[:, None]
        log_keep = jnp.where(past, jax.nn.log_sigmoid(-z), 0.0)
        between = lax.cumsum(log_keep, axis=3, reverse=True) - log_keep
        a = jnp.where(past, jnp.exp(jax.nn.log_sigmoid(z) + between), 0.0)
        outs.append(jnp.einsum('bhqk,bkhd->bqhd', a.astype(v.dtype), vb))
    return jnp.concatenate(outs, axis=1)


def chunked_relpos_attention(q, k, v, rel_bias):
    B, S, H, Dh = q.shape
    nc = S // CHUNK
    pad = LEFT_CHUNKS * CHUNK
    scale = Dh ** -0.5
    kp = jnp.pad(k, ((0, 0), (pad, 0), (0, 0), (0, 0)))
    vp = jnp.pad(v, ((0, 0), (pad, 0), (0, 0), (0, 0)))
    qc = jnp.moveaxis(q.reshape(B, nc, CHUNK, H, Dh), 1, 0)
    s_loc = jnp.arange(BAND)[None, :]
    rel = s_loc - (jnp.arange(CHUNK)[:, None] + pad)
    rel_idx = jnp.clip(rel, -REL_CLIP, CHUNK - 1) + REL_CLIP
    bias = rel_bias.astype(jnp.float32)[:, rel_idx]

    def one_chunk(args):
        c, qblk = args
        start = c * CHUNK
        kb = lax.dynamic_slice_in_dim(kp, start, BAND, axis=1)
        vb = lax.dynamic_slice_in_dim(vp, start, BAND, axis=1)
        valid = (start + s_loc) >= pad
        z = jnp.einsum('bqhd,bkhd->bhqk', qblk, kb).astype(jnp.float32) * scale + bias
        w = jax.nn.softmax(jnp.where(valid, z, NEG), axis=-1)
        return jnp.einsum('bhqk,bkhd->bqhd', w.astype(v.dtype), vb)

    out = lax.map(one_chunk, (jnp.arange(nc), qc))
    return jnp.moveaxis(out, 0, 1).reshape(B, S, H, Dh)


def setup_inputs(seed: int = 0) -> dict:
    key = jax.random.key(seed)
    ks = jax.random.split(key, 16)
    f32 = jnp.float32

    def w(k, shape, fan_in):
        return jax.random.normal(k, shape, f32) * fan_in ** -0.5

    def gain(k, shape):
        return 1.0 + 0.05 * jax.random.normal(k, shape, f32)

    return {
        "x": jax.random.normal(ks[0], (BATCH, SEQ, D_MODEL), f32),
        "p": jax.random.normal(ks[1], (DEPTH, BATCH, SEQ, D_PLE), f32),
        "w_in": w(ks[2], (DEPTH, D_MODEL, IN_COLS), D_MODEL),
        "w_sb_out": w(ks[3], (DEPTH, W_SB, D_MODEL), W_SB),
        "w_ca_out": w(ks[4], (DEPTH, W_CA, D_MODEL), W_CA),
        "w_mix_out": w(ks[5], (DEPTH, D_MODEL, D_MODEL), D_MODEL),
        "rel_bias": 0.3 * jax.random.normal(ks[6], (DEPTH, N_HEADS_CA, N_REL), f32),
        "g_mix": gain(ks[7], (DEPTH, D_MODEL)),
        "g_ffn": gain(ks[8], (DEPTH, D_MODEL)),
        "g_ple": gain(ks[9], (DEPTH, D_MODEL)),
        "g_final": gain(ks[10], (D_MODEL,)),
        "w_ffn_in": w(ks[11], (DEPTH, D_MODEL, 2 * D_FF), D_MODEL),
        "w_ffn_out": w(ks[12], (DEPTH, D_FF, D_MODEL), D_FF),
        "w_ple_in": w(ks[13], (DEPTH, D_PLE, D_MODEL), D_PLE),
        "w_ple_gate": w(ks[14], (DEPTH, D_MODEL, D_MODEL), D_MODEL),
    }


def reference(x, p, w_in, w_sb_out, w_ca_out, w_mix_out, rel_bias, g_mix, g_ffn, g_ple, g_final,
              w_ffn_in, w_ffn_out, w_ple_in, w_ple_gate):
    B, S, _ = x.shape
    splits = [W_SB, 2 * W_SB, 3 * W_SB, 3 * W_SB + W_CA, 3 * W_SB + 2 * W_CA, 3 * W_SB + 3 * W_CA,
              3 * W_SB + 3 * W_CA + D_MODEL]
    for i in range(DEPTH):
        h = rmsnorm(x, g_mix[i])
        proj = h @ w_in[i]
        q_sb, k_sb, v_sb, q_ca, k_ca, v_ca, gate_sb, gate_ca = jnp.split(proj, splits, axis=-1)
        y_sb = stick_breaking_attention(q_sb.reshape(B, S, N_HEADS_SB, HEAD_DIM),
                                        k_sb.reshape(B, S, N_HEADS_SB, HEAD_DIM),
                                        v_sb.reshape(B, S, N_HEADS_SB, HEAD_DIM)).reshape(B, S, W_SB)
        y_ca = chunked_relpos_attention(q_ca.reshape(B, S, N_HEADS_CA, HEAD_DIM),
                                        k_ca.reshape(B, S, N_HEADS_CA, HEAD_DIM),
                                        v_ca.reshape(B, S, N_HEADS_CA, HEAD_DIM),
                                        rel_bias[i]).reshape(B, S, W_CA)
        merged = (jax.nn.sigmoid(gate_sb) * (y_sb @ w_sb_out[i])
                  + jax.nn.sigmoid(gate_ca) * (y_ca @ w_ca_out[i]))
        x = x + merged @ w_mix_out[i]
        h = rmsnorm(x, g_ffn[i])
        g_ff, u_ff = jnp.split(h @ w_ffn_in[i], 2, axis=-1)
        x = x + (jax.nn.silu(g_ff) * u_ff) @ w_ffn_out[i]
        h = rmsnorm(x, g_ple[i])
        x = x + jax.nn.sigmoid(h @ w_ple_gate[i]) * (p[i] @ w_ple_in[i])
    return rmsnorm(x, g_final)
```

```python
import functools

import jax
import jax.numpy as jnp
from jax import lax
from jax.experimental import pallas as pl
from jax.experimental.pallas import tpu as pltpu

D_MODEL = 2048
CHUNK = 64
HEAD_DIM = 128
N_HEADS = 8
W_ATT = N_HEADS * HEAD_DIM
LEFT_CHUNKS = 8
REL_CLIP = 128
N_REL = REL_CLIP + CHUNK
D_FF = 5632
D_PLE = 256
EPS = 1e-6
NEG = -1e30
IN_COLS = 6 * W_ATT + 2 * D_MODEL

BF16 = jnp.bfloat16
F32 = jnp.float32

VMEM_LIMIT_BYTES = 52 * 1024 * 1024

TM_INPROJ, TN_INPROJ = 1024, 1024
TM_MERGE, TN_MERGE = 1024, 1024
TM_MIX, TN_MIX = 1024, 1024
TM_FFN, TF_FFN = 512, 512
TM_PLE = 512
SB_BLOCK = 128
CA_BLOCK = 128
CA_KEY_BLOCKS = LEFT_CHUNKS * CHUNK // CA_BLOCK + 1


def _params(*semantics):
    return pltpu.CompilerParams(dimension_semantics=semantics, vmem_limit_bytes=VMEM_LIMIT_BYTES)


def _rmsnorm_f32(x, g):
    ms = jnp.mean(x * x, axis=-1, keepdims=True)
    return x * lax.rsqrt(ms + EPS) * g


def _dot(a, b):
    return jnp.dot(a, b, preferred_element_type=F32)


def _dot_nt(a, b):
    return lax.dot_general(a, b, (((1,), (1,)), ((), ())), preferred_element_type=F32)


def _inproj_kernel(x_ref, g_ref, w_ref, o_ref, h_ref):
    @pl.when(pl.program_id(1) == 0)
    def _():
        h_ref[...] = _rmsnorm_f32(x_ref[...], g_ref[...]).astype(BF16)

    o_ref[...] = _dot(h_ref[...], w_ref[...]).astype(o_ref.dtype)


def _in_proj(x2d, g, w_bf16):
    t, d = x2d.shape
    n = w_bf16.shape[1]
    tm, tn = TM_INPROJ, TN_INPROJ
    return pl.pallas_call(
        _inproj_kernel,
        grid=(t // tm, n // tn),
        in_specs=[
            pl.BlockSpec((tm, d), lambda i, j: (i, 0)),
            pl.BlockSpec((1, d), lambda i, j: (0, 0)),
            pl.BlockSpec((d, tn), lambda i, j: (0, j)),
        ],
        out_specs=pl.BlockSpec((tm, tn), lambda i, j: (i, j)),
        out_shape=jax.ShapeDtypeStruct((t, n), BF16),
        scratch_shapes=[pltpu.VMEM((tm, d), BF16)],
        compiler_params=_params("parallel", "arbitrary"),
        name="in_proj",
    )(x2d, g, w_bf16)


def _sb_kernel(q_ref, k_ref, v_ref, o_ref):
    blk = SB_BLOCK
    i = pl.program_id(2)
    scale = HEAD_DIM ** -0.5
    q = q_ref[...]

    row = lax.broadcasted_iota(jnp.int32, (blk, blk), 0)
    col = lax.broadcasted_iota(jnp.int32, (blk, blk), 1)
    suffix = jnp.where(row > col, 1.0, 0.0).astype(BF16)
    past = col < row

    def block(j, carry, masked):
        start = pl.multiple_of(j * blk, blk)
        k = k_ref[pl.ds(start, blk), :]
        v = v_ref[pl.ds(start, blk), :]
        z = _dot_nt(q, k) * scale
        softplus_neg_abs = jnp.log(1.0 + jnp.exp(-jnp.abs(z)))
        log_keep = -(jnp.maximum(z, 0.0) + softplus_neg_abs)
        log_beta = jnp.minimum(z, 0.0) - softplus_neg_abs
        if masked:
            log_keep = jnp.where(past, log_keep, 0.0)
        hi = log_keep.astype(BF16)
        lo = (log_keep - hi.astype(F32)).astype(BF16)
        sums = _dot(jnp.concatenate([hi, lo], axis=0), suffix)
        between = sums[:blk] + sums[blk:] + carry
        a = jnp.exp(log_beta + between)
        if masked:
            a = jnp.where(past, a, 0.0)
        contrib = _dot(a.astype(BF16), v)
        new_carry = carry + jnp.sum(log_keep, axis=1, keepdims=True)
        return contrib, new_carry

    acc, carry = block(i, jnp.zeros((blk, 1), F32), True)

    def body(jj, state):
        acc, carry = state
        contrib, carry = block(i - jj, carry, False)
        return acc + contrib, carry

    acc, _ = lax.fori_loop(1, i + 1, body, (acc, carry))
    o_ref[...] = acc.astype(o_ref.dtype)


def _sb_attention(proj3d):
    b, s, _ = proj3d.shape
    blk = SB_BLOCK
    hb = W_ATT // HEAD_DIM
    return pl.pallas_call(
        _sb_kernel,
        grid=(b, N_HEADS, s // blk),
        in_specs=[
            pl.BlockSpec((None, blk, HEAD_DIM), lambda bi, h, i: (bi, i, h)),
            pl.BlockSpec((None, s, HEAD_DIM), lambda bi, h, i: (bi, 0, hb + h)),
            pl.BlockSpec((None, s, HEAD_DIM), lambda bi, h, i: (bi, 0, 2 * hb + h)),
        ],
        out_specs=pl.BlockSpec((None, blk, HEAD_DIM), lambda bi, h, i: (bi, i, h)),
        out_shape=jax.ShapeDtypeStruct((b, s, W_ATT), BF16),
        compiler_params=_params("parallel", "parallel", "arbitrary"),
        name="sb_attention",
    )(proj3d, proj3d, proj3d)


def _ca_bias_blocks(rel_bias):
    nb, blk = CA_KEY_BLOCKS, CA_BLOCK
    r = jnp.arange(2 * nb - 1)[:, None, None]
    t = jnp.arange(blk)[None, :, None]
    s = jnp.arange(blk)[None, None, :]
    rel = blk * (r - (nb - 1)) + s - t
    dchunk = (blk // CHUNK) * (r - (nb - 1)) + s // CHUNK - t // CHUNK
    valid = (dchunk >= -LEFT_CHUNKS) & (dchunk <= 0)
    idx = jnp.clip(rel, -REL_CLIP, CHUNK - 1) + REL_CLIP
    bias = rel_bias.astype(F32)[:, idx]
    return jnp.where(valid[None], bias, NEG)


def _ca_kernel(q_ref, k_ref, v_ref, t_ref, o_ref):
    blk, nb = CA_BLOCK, CA_KEY_BLOCKS
    i = pl.program_id(2)
    scale = HEAD_DIM ** -0.5
    first = jnp.maximum(i - (nb - 1), 0)
    shift = (nb - 1) - jnp.minimum(i, nb - 1)
    start = pl.multiple_of(first * blk, blk)
    k = k_ref[pl.ds(start, nb * blk), :]
    v = v_ref[pl.ds(start, nb * blk), :]
    bias = jnp.concatenate([t_ref[shift + r] for r in range(nb)], axis=1)
    z = _dot_nt(q_ref[...], k) * scale + bias
    m = jnp.max(z, axis=1, keepdims=True)
    e = jnp.exp(z - m)
    denom = jnp.sum(e, axis=1, keepdims=True)
    o_ref[...] = (_dot(e.astype(BF16), v) / denom).astype(o_ref.dtype)


def _ca_attention(proj3d, tables):
    b, s, _ = proj3d.shape
    blk = CA_BLOCK
    hb = W_ATT // HEAD_DIM
    nt = tables.shape[1]
    return pl.pallas_call(
        _ca_kernel,
        grid=(b, N_HEADS, s // blk),
        in_specs=[
            pl.BlockSpec((None, blk, HEAD_DIM), lambda bi, h, i: (bi, i, 3 * hb + h)),
            pl.BlockSpec((None, s, HEAD_DIM), lambda bi, h, i: (bi, 0, 4 * hb + h)),
            pl.BlockSpec((None, s, HEAD_DIM), lambda bi, h, i: (bi, 0, 5 * hb + h)),
            pl.BlockSpec((None, nt, blk, blk), lambda bi, h, i: (h, 0, 0, 0)),
        ],
        out_specs=pl.BlockSpec((None, blk, HEAD_DIM), lambda bi, h, i: (bi, i, h)),
        out_shape=jax.ShapeDtypeStruct((b, s, W_ATT), BF16),
        compiler_params=_params("parallel", "parallel", "arbitrary"),
        name="ca_attention",
    )(proj3d, proj3d, proj3d, tables)


def _merge_kernel(ysb_ref, yca_ref, wsb_ref, wca_ref, gsb_ref, gca_ref, o_ref):
    sb = _dot(ysb_ref[...], wsb_ref[...])
    ca = _dot(yca_ref[...], wca_ref[...])
    merged = (jax.nn.sigmoid(gsb_ref[...].astype(F32)) * sb
              + jax.nn.sigmoid(gca_ref[...].astype(F32)) * ca)
    o_ref[...] = merged.astype(o_ref.dtype)


def _merge(y_sb, y_ca, w_sb, w_ca, proj2d):
    t = y_sb.shape[0]
    tm, tn = TM_MERGE, TN_MERGE
    gate_sb_blk = 6 * W_ATT // tn
    gate_ca_blk = (6 * W_ATT + D_MODEL) // tn
    return pl.pallas_call(
        _merge_kernel,
        grid=(t // tm, D_MODEL // tn),
        in_specs=[
            pl.BlockSpec((tm, W_ATT), lambda i, j: (i, 0)),
            pl.BlockSpec((tm, W_ATT), lambda i, j: (i, 0)),
            pl.BlockSpec((W_ATT, tn), lambda i, j: (0, j)),
            pl.BlockSpec((W_ATT, tn), lambda i, j: (0, j)),
            pl.BlockSpec((tm, tn), lambda i, j: (i, gate_sb_blk + j)),
            pl.BlockSpec((tm, tn), lambda i, j: (i, gate_ca_blk + j)),
        ],
        out_specs=pl.BlockSpec((tm, tn), lambda i, j: (i, j)),
        out_shape=jax.ShapeDtypeStruct((t, D_MODEL), BF16),
        compiler_params=_params("parallel", "arbitrary"),
        name="merge",
    )(y_sb, y_ca, w_sb, w_ca, proj2d, proj2d)


def _mix_kernel(m_ref, w_ref, x_ref, o_ref):
    o_ref[...] = x_ref[...] + _dot(m_ref[...], w_ref[...])


def _mix_out(merged, w_mix, x2d):
    t = merged.shape[0]
    tm, tn = TM_MIX, TN_MIX
    return pl.pallas_call(
        _mix_kernel,
        grid=(t // tm, D_MODEL // tn),
        in_specs=[
            pl.BlockSpec((tm, D_MODEL), lambda i, j: (i, 0)),
            pl.BlockSpec((D_MODEL, tn), lambda i, j: (0, j)),
            pl.BlockSpec((tm, tn), lambda i, j: (i, j)),
        ],
        out_specs=pl.BlockSpec((tm, tn), lambda i, j: (i, j)),
        out_shape=jax.ShapeDtypeStruct((t, D_MODEL), F32),
        compiler_params=_params("parallel", "arbitrary"),
        name="mix_out",
    )(merged, w_mix, x2d)


def _ffn_kernel(x_ref, g_ref, wg_ref, wu_ref, wo_ref, o_ref, h_ref):
    @pl.when(pl.program_id(1) == 0)
    def _():
        x = x_ref[...]
        h_ref[...] = _rmsnorm_f32(x, g_ref[...]).astype(BF16)
        o_ref[...] = x

    h = h_ref[...]
    gate = _dot(h, wg_ref[...])
    up = _dot(h, wu_ref[...])
    act = (gate * jax.nn.sigmoid(gate) * up).astype(BF16)
    o_ref[...] += _dot(act, wo_ref[...])


def _ffn(x1, g, w_in, w_out):
    t = x1.shape[0]
    tm, tf = TM_FFN, TF_FFN
    nf = D_FF // tf
    return pl.pallas_call(
        _ffn_kernel,
        grid=(t // tm, nf),
        in_specs=[
            pl.BlockSpec((tm, D_MODEL), lambda i, f: (i, 0)),
            pl.BlockSpec((1, D_MODEL), lambda i, f: (0, 0)),
            pl.BlockSpec((D_MODEL, tf), lambda i, f: (0, f)),
            pl.BlockSpec((D_MODEL, tf), lambda i, f: (0, nf + f)),
            pl.BlockSpec((tf, D_MODEL), lambda i, f: (f, 0)),
        ],
        out_specs=pl.BlockSpec((tm, D_MODEL), lambda i, f: (i, 0)),
        out_shape=jax.ShapeDtypeStruct((t, D_MODEL), F32),
        scratch_shapes=[pltpu.VMEM((tm, D_MODEL), BF16)],
        compiler_params=_params("parallel", "arbitrary"),
        name="ffn",
    )(x1, g, w_in, w_in, w_out)


def _ple_kernel(x_ref, p_ref, gp_ref, gf_ref, wgate_ref, wple_ref, o_ref):
    x = x_ref[...]
    h = _rmsnorm_f32(x, gp_ref[...]).astype(BF16)
    gate = jax.nn.sigmoid(_dot(h, wgate_ref[...]))
    emb = _dot(p_ref[...].astype(BF16), wple_ref[...])
    o_ref[...] = _rmsnorm_f32(x + gate * emb, gf_ref[...])


def _ple_final(x2, p2d, g_ple, g_final, w_gate, w_ple):
    t = x2.shape[0]
    tm = TM_PLE
    return pl.pallas_call(
        _ple_kernel,
        grid=(t // tm,),
        in_specs=[
            pl.BlockSpec((tm, D_MODEL), lambda i: (i, 0)),
            pl.BlockSpec((tm, D_PLE), lambda i: (i, 0)),
            pl.BlockSpec((1, D_MODEL), lambda i: (0, 0)),
            pl.BlockSpec((1, D_MODEL), lambda i: (0, 0)),
            pl.BlockSpec((D_MODEL, D_MODEL), lambda i: (0, 0)),
            pl.BlockSpec((D_PLE, D_MODEL), lambda i: (0, 0)),
        ],
        out_specs=pl.BlockSpec((tm, D_MODEL), lambda i: (i, 0)),
        out_shape=jax.ShapeDtypeStruct((t, D_MODEL), F32),
        compiler_params=_params("parallel"),
        name="ple_final",
    )(x2, p2d, g_ple, g_final, w_gate, w_ple)


def kernel(x, p, w_in, w_sb_out, w_ca_out, w_mix_out, rel_bias, g_mix, g_ffn, g_ple, g_final,
           w_ffn_in, w_ffn_out, w_ple_in, w_ple_gate):
    b, s, d = x.shape
    assert w_in.shape[0] == 1, "the output norm is fused into the (single) layer's last kernel"
    xt = x.reshape(b * s, d)
    proj = _in_proj(xt, g_mix[0][None], w_in[0].astype(BF16))
    proj3d = proj.reshape(b, s, IN_COLS)
    y_sb = _sb_attention(proj3d).reshape(b * s, W_ATT)
    y_ca = _ca_attention(proj3d, _ca_bias_blocks(rel_bias[0])).reshape(b * s, W_ATT)
    merged = _merge(y_sb, y_ca, w_sb_out[0].astype(BF16), w_ca_out[0].astype(BF16), proj)
    xt = _mix_out(merged, w_mix_out[0].astype(BF16), xt)
    xt = _ffn(xt, g_ffn[0][None], w_ffn_in[0].astype(BF16), w_ffn_out[0].astype(BF16))
    xt = _ple_final(xt, p[0].reshape(b * s, D_PLE), g_ple[0][None], g_final[None],
                    w_ple_gate[0].astype(BF16), w_ple_in[0].astype(BF16))
    return xt.reshape(b, s, d)
```

```python
import functools

import jax
import jax.numpy as jnp
from jax import lax
from jax.experimental import pallas as pl
from jax.experimental.pallas import tpu as pltpu

D_MODEL = 2048
CHUNK = 64
HEAD_DIM = 128
N_HEADS = 8
W_ATT = N_HEADS * HEAD_DIM
LEFT_CHUNKS = 8
REL_CLIP = 128
N_REL = REL_CLIP + CHUNK
D_FF = 5632
D_PLE = 256
EPS = 1e-6
NEG = -1e30
IN_COLS = 6 * W_ATT + 2 * D_MODEL

BF16 = jnp.bfloat16
F32 = jnp.float32

VMEM_LIMIT_BYTES = 52 * 1024 * 1024

TM_INPROJ, TN_INPROJ = 1024, 1024
TM_MERGE, TN_MERGE = 1024, 1024
TM_MIX, TN_MIX = 1024, 1024
TM_FFN, TF_FFN = 512, 512
TM_PLE = 512
SB_Q_BLOCK, SB_K_BLOCK = 512, 256
CA_BLOCK = 128
CA_KEY_BLOCKS = LEFT_CHUNKS * CHUNK // CA_BLOCK + 1


def _params(*semantics):
    return pltpu.CompilerParams(dimension_semantics=semantics, vmem_limit_bytes=VMEM_LIMIT_BYTES)


def _rmsnorm_f32(x, g):
    ms = jnp.mean(x * x, axis=-1, keepdims=True)
    return x * lax.rsqrt(ms + EPS) * g


def _dot(a, b):
    return jnp.dot(a, b, preferred_element_type=F32)


def _dot_nt(a, b):
    return lax.dot_general(a, b, (((1,), (1,)), ((), ())), preferred_element_type=F32)


def _inproj_kernel(x_ref, g_ref, w_ref, o_ref, h_ref):
    @pl.when(pl.program_id(1) == 0)
    def _():
        h_ref[...] = _rmsnorm_f32(x_ref[...], g_ref[...]).astype(BF16)

    o_ref[...] = _dot(h_ref[...], w_ref[...]).astype(o_ref.dtype)


def _in_proj(x2d, g, w_bf16):
    t, d = x2d.shape
    n = w_bf16.shape[1]
    tm, tn = TM_INPROJ, TN_INPROJ
    return pl.pallas_call(
        _inproj_kernel,
        grid=(t // tm, n // tn),
        in_specs=[
            pl.BlockSpec((tm, d), lambda i, j: (i, 0)),
            pl.BlockSpec((1, d), lambda i, j: (0, 0)),
            pl.BlockSpec((d, tn), lambda i, j: (0, j)),
        ],
        out_specs=pl.BlockSpec((tm, tn), lambda i, j: (i, j)),
        out_shape=jax.ShapeDtypeStruct((t, n), BF16),
        scratch_shapes=[pltpu.VMEM((tm, d), BF16)],
        compiler_params=_params("parallel", "arbitrary"),
        name="in_proj",
    )(x2d, g, w_bf16)


def _sb_kernel(q_ref, k_ref, v_ref, o_ref):
    tq, tk = SB_Q_BLOCK, SB_K_BLOCK
    diag_blocks = tq // tk
    i = pl.program_id(2)
    scale = HEAD_DIM ** -0.5
    q = q_ref[...]

    row = lax.broadcasted_iota(jnp.int32, (tk, tk), 0)
    col = lax.broadcasted_iota(jnp.int32, (tk, tk), 1)
    suffix = jnp.where(row > col, 1.0, 0.0).astype(BF16)
    key_minus_query = (lax.broadcasted_iota(jnp.int32, (tq, tk), 1)
                       - lax.broadcasted_iota(jnp.int32, (tq, tk), 0))

    def block(j, acc, carry, masked):
        start = pl.multiple_of(j * tk, tk)
        k = k_ref[pl.ds(start, tk), :]
        v = v_ref[pl.ds(start, tk), :]
        z = _dot_nt(q, k) * scale
        softplus_neg_abs = jnp.log(1.0 + jnp.exp(-jnp.abs(z)))
        log_keep = -(jnp.maximum(z, 0.0) + softplus_neg_abs)
        log_beta = jnp.minimum(z, 0.0) - softplus_neg_abs
        if masked:
            past = key_minus_query < i * tq - j * tk
            log_keep = jnp.where(past, log_keep, 0.0)
        hi = log_keep.astype(BF16)
        lo = (log_keep - hi.astype(F32)).astype(BF16)
        sums = _dot(jnp.concatenate([hi, lo], axis=0), suffix)
        between = sums[:tq] + sums[tq:] + carry
        a = jnp.exp(log_beta + between)
        if masked:
            a = jnp.where(past, a, 0.0)
        acc = acc + _dot(a.astype(BF16), v)
        carry = carry + jnp.sum(log_keep, axis=1, keepdims=True)
        return acc, carry

    acc = jnp.zeros((tq, HEAD_DIM), F32)
    carry = jnp.zeros((tq, 1), F32)
    for d in range(diag_blocks):
        acc, carry = block((i + 1) * diag_blocks - 1 - d, acc, carry, True)

    def body(jj, state):
        return block(i * diag_blocks - 1 - jj, *state, False)

    acc, _ = lax.fori_loop(0, i * diag_blocks, body, (acc, carry))
    o_ref[...] = acc.astype(o_ref.dtype)


def _sb_attention(proj3d):
    b, s, _ = proj3d.shape
    blk = SB_Q_BLOCK
    hb = W_ATT // HEAD_DIM
    return pl.pallas_call(
        _sb_kernel,
        grid=(b, N_HEADS, s // blk),
        in_specs=[
            pl.BlockSpec((None, blk, HEAD_DIM), lambda bi, h, i: (bi, i, h)),
            pl.BlockSpec((None, s, HEAD_DIM), lambda bi, h, i: (bi, 0, hb + h)),
            pl.BlockSpec((None, s, HEAD_DIM), lambda bi, h, i: (bi, 0, 2 * hb + h)),
        ],
        out_specs=pl.BlockSpec((None, blk, HEAD_DIM), lambda bi, h, i: (bi, i, h)),
        out_shape=jax.ShapeDtypeStruct((b, s, W_ATT), BF16),
        compiler_params=_params("parallel", "parallel", "arbitrary"),
        name="sb_attention",
    )(proj3d, proj3d, proj3d)


def _ca_bias_table(rel_bias):
    blk, width = CA_BLOCK, CA_KEY_BLOCKS * CA_BLOCK
    left = LEFT_CHUNKS * CHUNK
    h = rel_bias.shape[0]
    rb = rel_bias.astype(F32)
    n_lo = left + blk - 1 - REL_CLIP
    n_hi = blk - CHUNK + 1
    f = jnp.concatenate([jnp.broadcast_to(rb[:, :1], (h, n_lo)), rb,
                         jnp.broadcast_to(rb[:, -1:], (h, n_hi))], axis=1)
    n = f.shape[1]
    skew = jnp.tile(f, (1, blk))[:, :blk * (n - 1)].reshape(h, blk, n - 1)
    toeplitz = skew[:, :, blk - 1:blk - 1 + width]
    t = jnp.arange(blk)[:, None]
    c = jnp.arange(width)[None, :]
    dchunk = c // CHUNK - LEFT_CHUNKS - t // CHUNK
    valid = (dchunk >= -LEFT_CHUNKS) & (dchunk <= 0)
    return jnp.where(valid[None], toeplitz, NEG)


def _ca_kernel(q_ref, k_ref, v_ref, t_ref, o_ref):
    blk, nb = CA_BLOCK, CA_KEY_BLOCKS
    scale = HEAD_DIM ** -0.5
    for i in range(q_ref.shape[0] // blk):
        first = max(0, i - (nb - 1))
        nk = i - first + 1
        rows = slice(i * blk, (i + 1) * blk)
        keys = slice(first * blk, (i + 1) * blk)
        bias = t_ref[:, (nb - nk) * blk:]
        z = _dot_nt(q_ref[rows, :], k_ref[keys, :]) * scale + bias
        m = jnp.max(z, axis=1, keepdims=True)
        e = jnp.exp(z - m)
        denom = jnp.sum(e, axis=1, keepdims=True)
        o_ref[rows, :] = (_dot(e.astype(BF16), v_ref[keys, :]) / denom).astype(o_ref.dtype)


def _ca_attention(proj3d, table):
    b, s, _ = proj3d.shape
    hb = W_ATT // HEAD_DIM

    def seq_spec(col0):
        return pl.BlockSpec((None, s, HEAD_DIM), lambda bi, h: (bi, 0, col0 + h))

    return pl.pallas_call(
        _ca_kernel,
        grid=(b, N_HEADS),
        in_specs=[
            seq_spec(3 * hb),
            seq_spec(4 * hb),
            seq_spec(5 * hb),
            pl.BlockSpec((None,) + table.shape[1:], lambda bi, h: (h, 0, 0)),
        ],
        out_specs=pl.BlockSpec((None, s, HEAD_DIM), lambda bi, h: (bi, 0, h)),
        out_shape=jax.ShapeDtypeStruct((b, s, W_ATT), BF16),
        compiler_params=_params("parallel", "parallel"),
        name="ca_attention",
    )(proj3d, proj3d, proj3d, table)


def _merge_kernel(ysb_ref, yca_ref, wsb_ref, wca_ref, gsb_ref, gca_ref, o_ref):
    sb = _dot(ysb_ref[...], wsb_ref[...])
    ca = _dot(yca_ref[...], wca_ref[...])
    merged = (jax.nn.sigmoid(gsb_ref[...].astype(F32)) * sb
              + jax.nn.sigmoid(gca_ref[...].astype(F32)) * ca)
    o_ref[...] = merged.astype(o_ref.dtype)


def _merge(y_sb, y_ca, w_sb, w_ca, proj2d):
    t = y_sb.shape[0]
    tm, tn = TM_MERGE, TN_MERGE
    gate_sb_blk = 6 * W_ATT // tn
    gate_ca_blk = (6 * W_ATT + D_MODEL) // tn
    return pl.pallas_call(
        _merge_kernel,
        grid=(t // tm, D_MODEL // tn),
        in_specs=[
            pl.BlockSpec((tm, W_ATT), lambda i, j: (i, 0)),
            pl.BlockSpec((tm, W_ATT), lambda i, j: (i, 0)),
            pl.BlockSpec((W_ATT, tn), lambda i, j: (0, j)),
            pl.BlockSpec((W_ATT, tn), lambda i, j: (0, j)),
            pl.BlockSpec((tm, tn), lambda i, j: (i, gate_sb_blk + j)),
            pl.BlockSpec((tm, tn), lambda i, j: (i, gate_ca_blk + j)),
        ],
        out_specs=pl.BlockSpec((tm, tn), lambda i, j: (i, j)),
        out_shape=jax.ShapeDtypeStruct((t, D_MODEL), BF16),
        compiler_params=_params("parallel", "arbitrary"),
        name="merge",
    )(y_sb, y_ca, w_sb, w_ca, proj2d, proj2d)


def _mix_kernel(m_ref, w_ref, x_ref, o_ref):
    o_ref[...] = x_ref[...] + _dot(m_ref[...], w_ref[...])


def _mix_out(merged, w_mix, x2d):
    t = merged.shape[0]
    tm, tn = TM_MIX, TN_MIX
    return pl.pallas_call(
        _mix_kernel,
        grid=(t // tm, D_MODEL // tn),
        in_specs=[
            pl.BlockSpec((tm, D_MODEL), lambda i, j: (i, 0)),
            pl.BlockSpec((D_MODEL, tn), lambda i, j: (0, j)),
            pl.BlockSpec((tm, tn), lambda i, j: (i, j)),
        ],
        out_specs=pl.BlockSpec((tm, tn), lambda i, j: (i, j)),
        out_shape=jax.ShapeDtypeStruct((t, D_MODEL), F32),
        compiler_params=_params("parallel", "arbitrary"),
        name="mix_out",
    )(merged, w_mix, x2d)


def _ffn_kernel(x_ref, g_ref, wg_ref, wu_ref, wo_ref, o_ref, h_ref):
    @pl.when(pl.program_id(1) == 0)
    def _():
        x = x_ref[...]
        h_ref[...] = _rmsnorm_f32(x, g_ref[...]).astype(BF16)
        o_ref[...] = x

    h = h_ref[...]
    gate = _dot(h, wg_ref[...])
    up = _dot(h, wu_ref[...])
    act = (gate * jax.nn.sigmoid(gate) * up).astype(BF16)
    o_ref[...] += _dot(act, wo_ref[...])


def _ffn(x1, g, w_in, w_out):
    t = x1.shape[0]
    tm, tf = TM_FFN, TF_FFN
    nf = D_FF // tf
    return pl.pallas_call(
        _ffn_kernel,
        grid=(t // tm, nf),
        in_specs=[
            pl.BlockSpec((tm, D_MODEL), lambda i, f: (i, 0)),
            pl.BlockSpec((1, D_MODEL), lambda i, f: (0, 0)),
            pl.BlockSpec((D_MODEL, tf), lambda i, f: (0, f)),
            pl.BlockSpec((D_MODEL, tf), lambda i, f: (0, nf + f)),
            pl.BlockSpec((tf, D_MODEL), lambda i, f: (f, 0)),
        ],
        out_specs=pl.BlockSpec((tm, D_MODEL), lambda i, f: (i, 0)),
        out_shape=jax.ShapeDtypeStruct((t, D_MODEL), F32),
        scratch_shapes=[pltpu.VMEM((tm, D_MODEL), BF16)],
        compiler_params=_params("parallel", "arbitrary"),
        name="ffn",
    )(x1, g, w_in, w_in, w_out)


def _ple_kernel(x_ref, p_ref, gp_ref, gf_ref, wgate_ref, wple_ref, o_ref):
    x = x_ref[...]
    h = _rmsnorm_f32(x, gp_ref[...]).astype(BF16)
    gate = jax.nn.sigmoid(_dot(h, wgate_ref[...]))
    emb = _dot(p_ref[...].astype(BF16), wple_ref[...])
    o_ref[...] = _rmsnorm_f32(x + gate * emb, gf_ref[...])


def _ple_final(x2, p2d, g_ple, g_final, w_gate, w_ple):
    t = x2.shape[0]
    tm = TM_PLE
    return pl.pallas_call(
        _ple_kernel,
        grid=(t // tm,),
        in_specs=[
            pl.BlockSpec((tm, D_MODEL), lambda i: (i, 0)),
            pl.BlockSpec((tm, D_PLE), lambda i: (i, 0)),
            pl.BlockSpec((1, D_MODEL), lambda i: (0, 0)),
            pl.BlockSpec((1, D_MODEL), lambda i: (0, 0)),
            pl.BlockSpec((D_MODEL, D_MODEL), lambda i: (0, 0)),
            pl.BlockSpec((D_PLE, D_MODEL), lambda i: (0, 0)),
        ],
        out_specs=pl.BlockSpec((tm, D_MODEL), lambda i: (i, 0)),
        out_shape=jax.ShapeDtypeStruct((t, D_MODEL), F32),
        compiler_params=_params("parallel"),
        name="ple_final",
    )(x2, p2d, g_ple, g_final, w_gate, w_ple)


def kernel(x, p, w_in, w_sb_out, w_ca_out, w_mix_out, rel_bias, g_mix, g_ffn, g_ple, g_final,
           w_ffn_in, w_ffn_out, w_ple_in, w_ple_gate):
    b, s, d = x.shape
    assert w_in.shape[0] == 1, "the output norm is fused into the (single) layer's last kernel"
    xt = x.reshape(b * s, d)
    proj = _in_proj(xt, g_mix[0][None], w_in[0].astype(BF16))
    proj3d = proj.reshape(b, s, IN_COLS)
    y_sb = _sb_attention(proj3d).reshape(b * s, W_ATT)
    y_ca = _ca_attention(proj3d, _ca_bias_table(rel_bias[0])).reshape(b * s, W_ATT)
    merged = _merge(y_sb, y_ca, w_sb_out[0].astype(BF16), w_ca_out[0].astype(BF16), proj)
    xt = _mix_out(merged, w_mix_out[0].astype(BF16), xt)
    xt = _ffn(xt, g_ffn[0][None], w_ffn_in[0].astype(BF16), w_ffn_out[0].astype(BF16))
    xt = _ple_final(xt, p[0].reshape(b * s, D_PLE), g_ple[0][None], g_final[None],
                    w_ple_gate[0].astype(BF16), w_ple_in[0].astype(BF16))
    return xt.reshape(b, s, d)
```

```python
import functools

import jax
import jax.numpy as jnp
from jax import lax
from jax.experimental import pallas as pl
from jax.experimental.pallas import tpu as pltpu

D_MODEL = 2048
CHUNK = 64
HEAD_DIM = 128
N_HEADS = 8
W_ATT = N_HEADS * HEAD_DIM
LEFT_CHUNKS = 8
REL_CLIP = 128
N_REL = REL_CLIP + CHUNK
D_FF = 5632
D_PLE = 256
EPS = 1e-6
NEG = -1e30
IN_COLS = 6 * W_ATT + 2 * D_MODEL

BF16 = jnp.bfloat16
F32 = jnp.float32

VMEM_LIMIT_BYTES = 52 * 1024 * 1024

TM_INPROJ, TN_INPROJ = 1024, 1024
TM_MERGE, TN_MERGE = 1024, 1024
TM_MIX = 512
TM_FFN, TF_FFN, TN_FFN_OUT = 1024, 512, 512
TM_PLE = 512
SB_Q_BLOCK, SB_K_BLOCK = 512, 256
SB_DEAD_LOG_WEIGHT = -110.0
CA_BLOCK = 128
CA_KEY_BLOCKS = LEFT_CHUNKS * CHUNK // CA_BLOCK + 1


def _params(*semantics):
    return pltpu.CompilerParams(dimension_semantics=semantics, vmem_limit_bytes=VMEM_LIMIT_BYTES)


def _rmsnorm_f32(x, g):
    ms = jnp.mean(x * x, axis=-1, keepdims=True)
    return x * lax.rsqrt(ms + EPS) * g


def _dot(a, b):
    return jnp.dot(a, b, preferred_element_type=F32)


def _dot_nt(a, b):
    return lax.dot_general(a, b, (((1,), (1,)), ((), ())), preferred_element_type=F32)


def _inproj_kernel(x_ref, g_ref, w_ref, o_ref, h_ref):
    @pl.when(pl.program_id(1) == 0)
    def _():
        h_ref[...] = _rmsnorm_f32(x_ref[...], g_ref[...]).astype(BF16)

    o_ref[...] = _dot(h_ref[...], w_ref[...]).astype(o_ref.dtype)


def _in_proj(x2d, g, w_bf16):
    t, d = x2d.shape
    n = w_bf16.shape[1]
    tm, tn = TM_INPROJ, TN_INPROJ
    return pl.pallas_call(
        _inproj_kernel,
        grid=(t // tm, n // tn),
        in_specs=[
            pl.BlockSpec((tm, d), lambda i, j: (i, 0)),
            pl.BlockSpec((1, d), lambda i, j: (0, 0)),
            pl.BlockSpec((d, tn), lambda i, j: (0, j)),
        ],
        out_specs=pl.BlockSpec((tm, tn), lambda i, j: (i, j)),
        out_shape=jax.ShapeDtypeStruct((t, n), BF16),
        scratch_shapes=[pltpu.VMEM((tm, d), BF16)],
        compiler_params=_params("parallel", "arbitrary"),
        name="in_proj",
    )(x2d, g, w_bf16)


def _sb_kernel(q_ref, k_ref, v_ref, o_ref):
    tq, tk = SB_Q_BLOCK, SB_K_BLOCK
    assert tq == 2 * tk
    i = pl.program_id(2)
    scale = HEAD_DIM ** -0.5

    row = lax.broadcasted_iota(jnp.int32, (tk, tk), 0)
    col = lax.broadcasted_iota(jnp.int32, (tk, tk), 1)
    suffix = jnp.where(row > col, 1.0, 0.0).astype(BF16)

    def block(q, j, acc, carry, diagonal):
        rows = q.shape[0]
        start = pl.multiple_of(j * tk, tk)
        k = k_ref[pl.ds(start, tk), :]
        v = v_ref[pl.ds(start, tk), :]
        z = _dot_nt(q, k) * scale
        softplus_neg_abs = jnp.log(1.0 + jnp.exp(-jnp.abs(z)))
        log_keep = -(jnp.maximum(z, 0.0) + softplus_neg_abs)
        log_beta = jnp.minimum(z, 0.0) - softplus_neg_abs
        if diagonal:
            past = (lax.broadcasted_iota(jnp.int32, (rows, tk), 1)
                    < lax.broadcasted_iota(jnp.int32, (rows, tk), 0))
            log_keep = jnp.where(past, log_keep, 0.0)
        hi = log_keep.astype(BF16)
        lo = (log_keep - hi.astype(F32)).astype(BF16)
        sums = _dot(jnp.concatenate([hi, lo], axis=0), suffix)
        between = sums[:rows] + sums[rows:] + carry
        a = jnp.exp(log_beta + between)
        if diagonal:
            a = jnp.where(past, a, 0.0)
        acc = acc + _dot(a.astype(BF16), v)
        carry = carry + jnp.sum(log_keep, axis=1, keepdims=True)
        return acc, carry

    acc_lo, carry_lo = block(q_ref[tk:, :], 2 * i + 1, jnp.zeros((tk, HEAD_DIM), F32),
                             jnp.zeros((tk, 1), F32), True)
    acc = jnp.concatenate([jnp.zeros((tk, HEAD_DIM), F32), acc_lo], axis=0)
    carry = jnp.concatenate([jnp.zeros((tk, 1), F32), carry_lo], axis=0)
    q = q_ref[...]
    acc, carry = block(q, 2 * i, acc, carry, True)

    def alive(carry):
        return jnp.max(carry) > SB_DEAD_LOG_WEIGHT

    def cond(state):
        j, live, _, _ = state
        return jnp.logical_and(j >= 0, live)

    def body(state):
        j, _, acc, carry = state
        acc, carry = block(q, j, acc, carry, False)
        return j - 1, alive(carry), acc, carry

    _, _, acc, _ = lax.while_loop(cond, body, (2 * i - 1, alive(carry), acc, carry))
    o_ref[...] = acc.astype(o_ref.dtype)


def _sb_attention(proj3d):
    b, s, _ = proj3d.shape
    blk = SB_Q_BLOCK
    hb = W_ATT // HEAD_DIM
    return pl.pallas_call(
        _sb_kernel,
        grid=(b, N_HEADS, s // blk),
        in_specs=[
            pl.BlockSpec((None, blk, HEAD_DIM), lambda bi, h, i: (bi, i, h)),
            pl.BlockSpec((None, s, HEAD_DIM), lambda bi, h, i: (bi, 0, hb + h)),
            pl.BlockSpec((None, s, HEAD_DIM), lambda bi, h, i: (bi, 0, 2 * hb + h)),
        ],
        out_specs=pl.BlockSpec((None, blk, HEAD_DIM), lambda bi, h, i: (bi, i, h)),
        out_shape=jax.ShapeDtypeStruct((b, s, W_ATT), BF16),
        compiler_params=_params("parallel", "parallel", "arbitrary"),
        name="sb_attention",
    )(proj3d, proj3d, proj3d)


def _ca_bias_table(rel_bias):
    blk, width = CA_BLOCK, CA_KEY_BLOCKS * CA_BLOCK
    left = LEFT_CHUNKS * CHUNK
    h = rel_bias.shape[0]
    rb = rel_bias.astype(F32)
    n_lo = left + blk - 1 - REL_CLIP
    n_hi = blk - CHUNK + 1
    f = jnp.concatenate([jnp.broadcast_to(rb[:, :1], (h, n_lo)), rb,
                         jnp.broadcast_to(rb[:, -1:], (h, n_hi))], axis=1)
    n = f.shape[1]
    skew = jnp.tile(f, (1, blk))[:, :blk * (n - 1)].reshape(h, blk, n - 1)
    toeplitz = skew[:, :, blk - 1:blk - 1 + width]
    t = jnp.arange(blk)[:, None]
    c = jnp.arange(width)[None, :]
    dchunk = c // CHUNK - LEFT_CHUNKS - t // CHUNK
    valid = (dchunk >= -LEFT_CHUNKS) & (dchunk <= 0)
    return jnp.where(valid[None], toeplitz, NEG)


def _ca_kernel(q_ref, k_ref, v_ref, t_ref, o_ref):
    blk, nb = CA_BLOCK, CA_KEY_BLOCKS
    scale = HEAD_DIM ** -0.5
    for i in range(q_ref.shape[0] // blk):
        first = max(0, i - (nb - 1))
        nk = i - first + 1
        rows = slice(i * blk, (i + 1) * blk)
        keys = slice(first * blk, (i + 1) * blk)
        bias = t_ref[:, (nb - nk) * blk:]
        z = _dot_nt(q_ref[rows, :], k_ref[keys, :]) * scale + bias
        m = jnp.max(z, axis=1, keepdims=True)
        e = jnp.exp(z - m)
        denom = jnp.sum(e, axis=1, keepdims=True)
        o_ref[rows, :] = (_dot(e.astype(BF16), v_ref[keys, :]) / denom).astype(o_ref.dtype)


def _ca_attention(proj3d, table):
    b, s, _ = proj3d.shape
    hb = W_ATT // HEAD_DIM

    def seq_spec(col0):
        return pl.BlockSpec((None, s, HEAD_DIM), lambda bi, h: (bi, 0, col0 + h))

    return pl.pallas_call(
        _ca_kernel,
        grid=(b, N_HEADS),
        in_specs=[
            seq_spec(3 * hb),
            seq_spec(4 * hb),
            seq_spec(5 * hb),
            pl.BlockSpec((None,) + table.shape[1:], lambda bi, h: (h, 0, 0)),
        ],
        out_specs=pl.BlockSpec((None, s, HEAD_DIM), lambda bi, h: (bi, 0, h)),
        out_shape=jax.ShapeDtypeStruct((b, s, W_ATT), BF16),
        compiler_params=_params("parallel", "parallel"),
        name="ca_attention",
    )(proj3d, proj3d, proj3d, table)


def _merge_kernel(ysb_ref, yca_ref, wsb_ref, wca_ref, gsb_ref, gca_ref, o_ref):
    sb = _dot(ysb_ref[...], wsb_ref[...])
    ca = _dot(yca_ref[...], wca_ref[...])
    merged = (jax.nn.sigmoid(gsb_ref[...].astype(F32)) * sb
              + jax.nn.sigmoid(gca_ref[...].astype(F32)) * ca)
    o_ref[...] = merged.astype(o_ref.dtype)


def _merge(y_sb, y_ca, w_sb, w_ca, proj2d):
    t = y_sb.shape[0]
    tm, tn = TM_MERGE, TN_MERGE
    gate_sb_blk = 6 * W_ATT // tn
    gate_ca_blk = (6 * W_ATT + D_MODEL) // tn
    return pl.pallas_call(
        _merge_kernel,
        grid=(t // tm, D_MODEL // tn),
        in_specs=[
            pl.BlockSpec((tm, W_ATT), lambda i, j: (i, 0)),
            pl.BlockSpec((tm, W_ATT), lambda i, j: (i, 0)),
            pl.BlockSpec((W_ATT, tn), lambda i, j: (0, j)),
            pl.BlockSpec((W_ATT, tn), lambda i, j: (0, j)),
            pl.BlockSpec((tm, tn), lambda i, j: (i, gate_sb_blk + j)),
            pl.BlockSpec((tm, tn), lambda i, j: (i, gate_ca_blk + j)),
        ],
        out_specs=pl.BlockSpec((tm, tn), lambda i, j: (i, j)),
        out_shape=jax.ShapeDtypeStruct((t, D_MODEL), BF16),
        compiler_params=_params("parallel", "arbitrary"),
        name="merge",
    )(y_sb, y_ca, w_sb, w_ca, proj2d, proj2d)


def _resident(shape):
    return pl.BlockSpec(shape, lambda *_: (0,) * len(shape), pipeline_mode=pl.Buffered(1))


def _mix_kernel(m_ref, w_ref, x_ref, g_ref, x1_ref, h_ref):
    x1 = x_ref[...] + _dot(m_ref[...], w_ref[...])
    x1_ref[...] = x1
    h_ref[...] = _rmsnorm_f32(x1, g_ref[...]).astype(BF16)


def _mix_out(merged, w_mix, x2d, g_ffn):
    t = merged.shape[0]
    tm = TM_MIX
    row_block = pl.BlockSpec((tm, D_MODEL), lambda i: (i, 0))
    return pl.pallas_call(
        _mix_kernel,
        grid=(t // tm,),
        in_specs=[row_block, _resident((D_MODEL, D_MODEL)), row_block, _resident((1, D_MODEL))],
        out_specs=[row_block, row_block],
        out_shape=[jax.ShapeDtypeStruct((t, D_MODEL), F32), jax.ShapeDtypeStruct((t, D_MODEL), BF16)],
        compiler_params=_params("parallel"),
        name="mix_out",
    )(merged, w_mix, x2d, g_ffn)


def _ffn_kernel(h_ref, wg_ref, wu_ref, wo_ref, o_ref):
    @pl.when(pl.program_id(1) == 0)
    def _():
        o_ref[...] = jnp.zeros_like(o_ref)

    h = h_ref[...]
    gate = _dot(h, wg_ref[...])
    up = _dot(h, wu_ref[...])
    act = (gate * jax.nn.sigmoid(gate) * up).astype(BF16)
    for c in range(0, D_MODEL, TN_FFN_OUT):
        cols = slice(c, c + TN_FFN_OUT)
        o_ref[:, cols] += _dot(act, wo_ref[:, cols])


def _ffn(h, w_in, w_out):
    t = h.shape[0]
    tm, tf = TM_FFN, TF_FFN
    nf = D_FF // tf
    return pl.pallas_call(
        _ffn_kernel,
        grid=(t // tm, nf),
        in_specs=[
            pl.BlockSpec((tm, D_MODEL), lambda i, f: (i, 0)),
            pl.BlockSpec((D_MODEL, tf), lambda i, f: (0, f)),
            pl.BlockSpec((D_MODEL, tf), lambda i, f: (0, nf + f)),
            pl.BlockSpec((tf, D_MODEL), lambda i, f: (f, 0)),
        ],
        out_specs=pl.BlockSpec((tm, D_MODEL), lambda i, f: (i, 0)),
        out_shape=jax.ShapeDtypeStruct((t, D_MODEL), F32),
        compiler_params=_params("parallel", "arbitrary"),
        name="ffn",
    )(h, w_in, w_in, w_out)


def _ple_kernel(x1_ref, y_ref, p_ref, gp_ref, gf_ref, wgate_ref, wple_ref, o_ref):
    x = x1_ref[...] + y_ref[...]
    h = _rmsnorm_f32(x, gp_ref[...]).astype(BF16)
    gate = jax.nn.sigmoid(_dot(h, wgate_ref[...]))
    emb = _dot(p_ref[...].astype(BF16), wple_ref[...])
    o_ref[...] = _rmsnorm_f32(x + gate * emb, gf_ref[...])


def _ple_final(x1, y_ffn, p2d, g_ple, g_final, w_gate, w_ple):
    t = x1.shape[0]
    tm = TM_PLE
    row_block = pl.BlockSpec((tm, D_MODEL), lambda i: (i, 0))
    return pl.pallas_call(
        _ple_kernel,
        grid=(t // tm,),
        in_specs=[
            row_block,
            row_block,
            pl.BlockSpec((tm, D_PLE), lambda i: (i, 0)),
            _resident((1, D_MODEL)),
            _resident((1, D_MODEL)),
            _resident((D_MODEL, D_MODEL)),
            _resident((D_PLE, D_MODEL)),
        ],
        out_specs=row_block,
        out_shape=jax.ShapeDtypeStruct((t, D_MODEL), F32),
        compiler_params=_params("parallel"),
        name="ple_final",
    )(x1, y_ffn, p2d, g_ple, g_final, w_gate, w_ple)


def kernel(x, p, w_in, w_sb_out, w_ca_out, w_mix_out, rel_bias, g_mix, g_ffn, g_ple, g_final,
           w_ffn_in, w_ffn_out, w_ple_in, w_ple_gate):
    b, s, d = x.shape
    assert w_in.shape[0] == 1, "the output norm is fused into the (single) layer's last kernel"
    xt = x.reshape(b * s, d)
    proj = _in_proj(xt, g_mix[0][None], w_in[0].astype(BF16))
    proj3d = proj.reshape(b, s, IN_COLS)
    y_sb = _sb_attention(proj3d).reshape(b * s, W_ATT)
    y_ca = _ca_attention(proj3d, _ca_bias_table(rel_bias[0])).reshape(b * s, W_ATT)
    merged = _merge(y_sb, y_ca, w_sb_out[0].astype(BF16), w_ca_out[0].astype(BF16), proj)
    x1, h_ffn = _mix_out(merged, w_mix_out[0].astype(BF16), xt, g_ffn[0][None])
    y_ffn = _ffn(h_ffn, w_ffn_in[0].astype(BF16), w_ffn_out[0].astype(BF16))
    out = _ple_final(x1, y_ffn, p[0].reshape(b * s, D_PLE), g_ple[0][None], g_final[None],
                     w_ple_gate[0].astype(BF16), w_ple_in[0].astype(BF16))
    return out.reshape(b, s, d)
```

```python
import functools

import jax
import jax.numpy as jnp
from jax import lax
from jax.experimental import pallas as pl
from jax.experimental.pallas import tpu as pltpu

D_MODEL = 2048
CHUNK = 64
HEAD_DIM = 128
N_HEADS = 8
W_ATT = N_HEADS * HEAD_DIM
LEFT_CHUNKS = 8
REL_CLIP = 128
N_REL = REL_CLIP + CHUNK
D_FF = 5632
D_PLE = 256
EPS = 1e-6
NEG = -1e30
IN_COLS = 6 * W_ATT + 2 * D_MODEL

BF16 = jnp.bfloat16
F32 = jnp.float32

VMEM_LIMIT_BYTES = 52 * 1024 * 1024

TM_INPROJ, TN_INPROJ = 1024, 1024
TM_MERGE, TN_MERGE = 1024, 1024
TM_MIX = 512
TM_FFN, TF_FFN, TN_FFN_OUT = 1024, 512, 512
TM_PLE = 512
SB_Q_BLOCK, SB_K_BLOCK = 512, 256
SB_DEAD_LOG_WEIGHT = -110.0
SB_HEADS_PER_STEP = 2
CA_BLOCK = 128
CA_KEY_BLOCKS = LEFT_CHUNKS * CHUNK // CA_BLOCK + 1
CA_GROUP = 4


def _params(*semantics):
    return pltpu.CompilerParams(dimension_semantics=semantics, vmem_limit_bytes=VMEM_LIMIT_BYTES)


def _rmsnorm_f32(x, g):
    ms = jnp.mean(x * x, axis=-1, keepdims=True)
    return x * lax.rsqrt(ms + EPS) * g


def _dot(a, b):
    return jnp.dot(a, b, preferred_element_type=F32)


def _dot_nt(a, b):
    return lax.dot_general(a, b, (((1,), (1,)), ((), ())), preferred_element_type=F32)


def _inproj_kernel(x_ref, g_ref, w_ref, o_ref, h_ref):
    @pl.when(pl.program_id(1) == 0)
    def _():
        h_ref[...] = _rmsnorm_f32(x_ref[...], g_ref[...]).astype(BF16)

    o_ref[...] = _dot(h_ref[...], w_ref[...]).astype(o_ref.dtype)


def _in_proj(x2d, g, w_bf16):
    t, d = x2d.shape
    n = w_bf16.shape[1]
    tm, tn = TM_INPROJ, TN_INPROJ
    return pl.pallas_call(
        _inproj_kernel,
        grid=(t // tm, n // tn),
        in_specs=[
            pl.BlockSpec((tm, d), lambda i, j: (i, 0)),
            pl.BlockSpec((1, d), lambda i, j: (0, 0)),
            pl.BlockSpec((d, tn), lambda i, j: (0, j)),
        ],
        out_specs=pl.BlockSpec((tm, tn), lambda i, j: (i, j)),
        out_shape=jax.ShapeDtypeStruct((t, n), BF16),
        scratch_shapes=[pltpu.VMEM((tm, d), BF16)],
        compiler_params=_params("parallel", "arbitrary"),
        name="in_proj",
    )(x2d, g, w_bf16)


def _sb_kernel(q_ref, k_ref, v_ref, o_ref):
    tq, tk = SB_Q_BLOCK, SB_K_BLOCK
    assert tq == 2 * tk
    heads = range(SB_HEADS_PER_STEP)
    i = pl.program_id(2)
    scale = HEAD_DIM ** -0.5

    def cols(h):
        return slice(h * HEAD_DIM, (h + 1) * HEAD_DIM)

    row = lax.broadcasted_iota(jnp.int32, (2 * tk, tk), 0)
    col = lax.broadcasted_iota(jnp.int32, (2 * tk, tk), 1)
    suffix2 = jnp.where((row & (tk - 1)) > col, 1.0, 0.0).astype(BF16)

    def block(qs, j, accs, carries, diagonal):
        rows = qs[0].shape[0]
        start = pl.multiple_of(j * tk, tk)
        zs = [_dot_nt(q, k_ref[pl.ds(start, tk), cols(h)]) * scale for h, q in zip(heads, qs)]
        if diagonal:
            past = (lax.broadcasted_iota(jnp.int32, (rows, tk), 1)
                    < lax.broadcasted_iota(jnp.int32, (rows, tk), 0))
        log_betas, log_keeps, pieces = [], [], []
        for z in zs:
            log_beta = jnp.minimum(z, 0.0) - jnp.log(1.0 + jnp.exp(-jnp.abs(z)))
            log_keep = log_beta - z
            if diagonal:
                log_keep = jnp.where(past, log_keep, 0.0)
            hi = log_keep.astype(BF16)
            lo = (log_keep - hi.astype(F32)).astype(BF16)
            log_betas.append(log_beta)
            log_keeps.append(log_keep)
            pieces.append(jnp.concatenate([hi, lo], axis=1))
        sums = [_dot(p, suffix2) for p in pieces]
        weights = []
        for log_beta, between, carry in zip(log_betas, sums, carries):
            a = jnp.exp(log_beta + between + carry)
            if diagonal:
                a = jnp.where(past, a, 0.0)
            weights.append(a.astype(BF16))
        accs = [acc + _dot(a, v_ref[pl.ds(start, tk), cols(h)]) for h, a, acc in zip(heads, weights, accs)]
        carries = [c + jnp.sum(lk, axis=1, keepdims=True) for c, lk in zip(carries, log_keeps)]
        return accs, carries

    accs, carries = block([q_ref[tk:, cols(h)] for h in heads], 2 * i + 1,
                          [jnp.zeros((tk, HEAD_DIM), F32) for _ in heads],
                          [jnp.zeros((tk, 1), F32) for _ in heads], True)
    accs = [jnp.concatenate([jnp.zeros((tk, HEAD_DIM), F32), a], axis=0) for a in accs]
    carries = [jnp.concatenate([jnp.zeros((tk, 1), F32), c], axis=0) for c in carries]
    qs = [q_ref[:, cols(h)] for h in heads]
    accs, carries = block(qs, 2 * i, accs, carries, True)

    def alive(carries):
        return jnp.max(functools.reduce(jnp.maximum, carries)) > SB_DEAD_LOG_WEIGHT

    def cond(state):
        j, live, _, _ = state
        return jnp.logical_and(j >= 0, live)

    def body(state):
        j, _, accs, carries = state
        accs, carries = block(qs, j, accs, carries, False)
        return j - 1, alive(carries), accs, carries

    _, _, accs, _ = lax.while_loop(cond, body, (2 * i - 1, alive(carries), accs, carries))
    for h, acc in zip(heads, accs):
        o_ref[:, cols(h)] = acc.astype(o_ref.dtype)


def _sb_attention(proj3d):
    b, s, _ = proj3d.shape
    blk = SB_Q_BLOCK
    width = SB_HEADS_PER_STEP * HEAD_DIM
    groups = W_ATT // width
    return pl.pallas_call(
        _sb_kernel,
        grid=(b, groups, s // blk),
        in_specs=[
            pl.BlockSpec((None, blk, width), lambda bi, g, i: (bi, i, g)),
            pl.BlockSpec((None, s, width), lambda bi, g, i: (bi, 0, groups + g)),
            pl.BlockSpec((None, s, width), lambda bi, g, i: (bi, 0, 2 * groups + g)),
        ],
        out_specs=pl.BlockSpec((None, blk, width), lambda bi, g, i: (bi, i, g)),
        out_shape=jax.ShapeDtypeStruct((b, s, W_ATT), BF16),
        compiler_params=_params("parallel", "parallel", "arbitrary"),
        name="sb_attention",
    )(proj3d, proj3d, proj3d)


def _ca_bias_table(rel_bias):
    blk, width = CA_BLOCK, CA_KEY_BLOCKS * CA_BLOCK
    left = LEFT_CHUNKS * CHUNK
    h = rel_bias.shape[0]
    rb = rel_bias.astype(F32)
    n_lo = left + blk - 1 - REL_CLIP
    n_hi = blk - CHUNK + 1
    f = jnp.concatenate([jnp.broadcast_to(rb[:, :1], (h, n_lo)), rb,
                         jnp.broadcast_to(rb[:, -1:], (h, n_hi))], axis=1)
    n = f.shape[1]
    skew = jnp.tile(f, (1, blk))[:, :blk * (n - 1)].reshape(h, blk, n - 1)
    toeplitz = skew[:, :, blk - 1:blk - 1 + width]
    t = jnp.arange(blk)[:, None]
    c = jnp.arange(width)[None, :]
    dchunk = c // CHUNK - LEFT_CHUNKS - t // CHUNK
    valid = (dchunk >= -LEFT_CHUNKS) & (dchunk <= 0)
    return jnp.where(valid[None], toeplitz, NEG)


def _ca_kernel(q_ref, k_ref, v_ref, t_ref, o_ref):
    blk, nb = CA_BLOCK, CA_KEY_BLOCKS
    scale = HEAD_DIM ** -0.5

    def rows(i):
        return slice(i * blk, (i + 1) * blk)

    def keys(i):
        return slice(max(0, i - (nb - 1)) * blk, (i + 1) * blk)

    def scores(i):
        nk = i - max(0, i - (nb - 1)) + 1
        return _dot_nt(q_ref[rows(i), :], k_ref[keys(i), :]) * scale + t_ref[:, (nb - nk) * blk:]

    n_blocks = q_ref.shape[0] // blk
    for g in range(0, n_blocks, CA_GROUP):
        group = range(g, g + CA_GROUP)
        zs = [scores(i) for i in group]
        ms = [jnp.max(z, axis=1, keepdims=True) for z in zs]
        es = [jnp.exp(z - m) for z, m in zip(zs, ms)]
        denoms = [jnp.sum(e, axis=1, keepdims=True) for e in es]
        outs = [_dot(e.astype(BF16), v_ref[keys(i), :]) for e, i in zip(es, group)]
        for i, out, denom in zip(group, outs, denoms):
            o_ref[rows(i), :] = (out / denom).astype(o_ref.dtype)


def _ca_attention(proj3d, table):
    b, s, _ = proj3d.shape
    hb = W_ATT // HEAD_DIM

    def seq_spec(col0):
        return pl.BlockSpec((None, s, HEAD_DIM), lambda bi, h: (bi, 0, col0 + h))

    return pl.pallas_call(
        _ca_kernel,
        grid=(b, N_HEADS),
        in_specs=[
            seq_spec(3 * hb),
            seq_spec(4 * hb),
            seq_spec(5 * hb),
            pl.BlockSpec((None,) + table.shape[1:], lambda bi, h: (h, 0, 0)),
        ],
        out_specs=pl.BlockSpec((None, s, HEAD_DIM), lambda bi, h: (bi, 0, h)),
        out_shape=jax.ShapeDtypeStruct((b, s, W_ATT), BF16),
        compiler_params=_params("parallel", "parallel"),
        name="ca_attention",
    )(proj3d, proj3d, proj3d, table)


def _merge_kernel(ysb_ref, yca_ref, wsb_ref, wca_ref, gsb_ref, gca_ref, o_ref):
    sb = _dot(ysb_ref[...], wsb_ref[...])
    ca = _dot(yca_ref[...], wca_ref[...])
    merged = (jax.nn.sigmoid(gsb_ref[...].astype(F32)) * sb
              + jax.nn.sigmoid(gca_ref[...].astype(F32)) * ca)
    o_ref[...] = merged.astype(o_ref.dtype)


def _merge(y_sb, y_ca, w_sb, w_ca, proj2d):
    t = y_sb.shape[0]
    tm, tn = TM_MERGE, TN_MERGE
    gate_sb_blk = 6 * W_ATT // tn
    gate_ca_blk = (6 * W_ATT + D_MODEL) // tn
    return pl.pallas_call(
        _merge_kernel,
        grid=(t // tm, D_MODEL // tn),
        in_specs=[
            pl.BlockSpec((tm, W_ATT), lambda i, j: (i, 0)),
            pl.BlockSpec((tm, W_ATT), lambda i, j: (i, 0)),
            pl.BlockSpec((W_ATT, tn), lambda i, j: (0, j)),
            pl.BlockSpec((W_ATT, tn), lambda i, j: (0, j)),
            pl.BlockSpec((tm, tn), lambda i, j: (i, gate_sb_blk + j)),
            pl.BlockSpec((tm, tn), lambda i, j: (i, gate_ca_blk + j)),
        ],
        out_specs=pl.BlockSpec((tm, tn), lambda i, j: (i, j)),
        out_shape=jax.ShapeDtypeStruct((t, D_MODEL), BF16),
        compiler_params=_params("parallel", "arbitrary"),
        name="merge",
    )(y_sb, y_ca, w_sb, w_ca, proj2d, proj2d)


def _resident(shape):
    return pl.BlockSpec(shape, lambda *_: (0,) * len(shape), pipeline_mode=pl.Buffered(1))


def _mix_kernel(m_ref, w_ref, x_ref, g_ref, x1_ref, h_ref):
    x1 = x_ref[...] + _dot(m_ref[...], w_ref[...])
    x1_ref[...] = x1
    h_ref[...] = _rmsnorm_f32(x1, g_ref[...]).astype(BF16)


def _mix_out(merged, w_mix, x2d, g_ffn):
    t = merged.shape[0]
    tm = TM_MIX
    row_block = pl.BlockSpec((tm, D_MODEL), lambda i: (i, 0))
    return pl.pallas_call(
        _mix_kernel,
        grid=(t // tm,),
        in_specs=[row_block, _resident((D_MODEL, D_MODEL)), row_block, _resident((1, D_MODEL))],
        out_specs=[row_block, row_block],
        out_shape=[jax.ShapeDtypeStruct((t, D_MODEL), F32), jax.ShapeDtypeStruct((t, D_MODEL), BF16)],
        compiler_params=_params("parallel"),
        name="mix_out",
    )(merged, w_mix, x2d, g_ffn)


def _ffn_kernel(h_ref, wg_ref, wu_ref, wo_ref, o_ref):
    @pl.when(pl.program_id(1) == 0)
    def _():
        o_ref[...] = jnp.zeros_like(o_ref)

    h = h_ref[...]
    gate = _dot(h, wg_ref[...])
    up = _dot(h, wu_ref[...])
    act = (gate * jax.nn.sigmoid(gate) * up).astype(BF16)
    for c in range(0, D_MODEL, TN_FFN_OUT):
        cols = slice(c, c + TN_FFN_OUT)
        o_ref[:, cols] += _dot(act, wo_ref[:, cols])


def _ffn(h, w_in, w_out):
    t = h.shape[0]
    tm, tf = TM_FFN, TF_FFN
    nf = D_FF // tf
    return pl.pallas_call(
        _ffn_kernel,
        grid=(t // tm, nf),
        in_specs=[
            pl.BlockSpec((tm, D_MODEL), lambda i, f: (i, 0)),
            pl.BlockSpec((D_MODEL, tf), lambda i, f: (0, f)),
            pl.BlockSpec((D_MODEL, tf), lambda i, f: (0, nf + f)),
            pl.BlockSpec((tf, D_MODEL), lambda i, f: (f, 0)),
        ],
        out_specs=pl.BlockSpec((tm, D_MODEL), lambda i, f: (i, 0)),
        out_shape=jax.ShapeDtypeStruct((t, D_MODEL), F32),
        compiler_params=_params("parallel", "arbitrary"),
        name="ffn",
    )(h, w_in, w_in, w_out)


def _ple_kernel(x1_ref, y_ref, p_ref, gp_ref, gf_ref, wgate_ref, wple_ref, o_ref):
    x = x1_ref[...] + y_ref[...]
    h = _rmsnorm_f32(x, gp_ref[...]).astype(BF16)
    gate = jax.nn.sigmoid(_dot(h, wgate_ref[...]))
    emb = _dot(p_ref[...].astype(BF16), wple_ref[...])
    o_ref[...] = _rmsnorm_f32(x + gate * emb, gf_ref[...])


def _ple_final(x1, y_ffn, p2d, g_ple, g_final, w_gate, w_ple):
    t = x1.shape[0]
    tm = TM_PLE
    row_block = pl.BlockSpec((tm, D_MODEL), lambda i: (i, 0))
    return pl.pallas_call(
        _ple_kernel,
        grid=(t // tm,),
        in_specs=[
            row_block,
            row_block,
            pl.BlockSpec((tm, D_PLE), lambda i: (i, 0)),
            _resident((1, D_MODEL)),
            _resident((1, D_MODEL)),
            _resident((D_MODEL, D_MODEL)),
            _resident((D_PLE, D_MODEL)),
        ],
        out_specs=row_block,
        out_shape=jax.ShapeDtypeStruct((t, D_MODEL), F32),
        compiler_params=_params("parallel"),
        name="ple_final",
    )(x1, y_ffn, p2d, g_ple, g_final, w_gate, w_ple)


def kernel(x, p, w_in, w_sb_out, w_ca_out, w_mix_out, rel_bias, g_mix, g_ffn, g_ple, g_final,
           w_ffn_in, w_ffn_out, w_ple_in, w_ple_gate):
    b, s, d = x.shape
    assert w_in.shape[0] == 1, "the output norm is fused into the (single) layer's last kernel"
    xt = x.reshape(b * s, d)
    proj = _in_proj(xt, g_mix[0][None], w_in[0].astype(BF16))
    proj3d = proj.reshape(b, s, IN_COLS)
    y_sb = _sb_attention(proj3d).reshape(b * s, W_ATT)
    y_ca = _ca_attention(proj3d, _ca_bias_table(rel_bias[0])).reshape(b * s, W_ATT)
    merged = _merge(y_sb, y_ca, w_sb_out[0].astype(BF16), w_ca_out[0].astype(BF16), proj)
    x1, h_ffn = _mix_out(merged, w_mix_out[0].astype(BF16), xt, g_ffn[0][None])
    y_ffn = _ffn(h_ffn, w_ffn_in[0].astype(BF16), w_ffn_out[0].astype(BF16))
    out = _ple_final(x1, y_ffn, p[0].reshape(b * s, D_PLE), g_ple[0][None], g_final[None],
                     w_ple_gate[0].astype(BF16), w_ple_in[0].astype(BF16))
    return out.reshape(b, s, d)
```

```python
import functools

import jax
import jax.numpy as jnp
from jax import lax
from jax.experimental import pallas as pl
from jax.experimental.pallas import tpu as pltpu

D_MODEL = 2048
CHUNK = 64
HEAD_DIM = 128
N_HEADS = 8
W_ATT = N_HEADS * HEAD_DIM
LEFT_CHUNKS = 8
REL_CLIP = 128
N_REL = REL_CLIP + CHUNK
D_FF = 5632
D_PLE = 256
EPS = 1e-6
NEG = -1e30
IN_COLS = 6 * W_ATT + 2 * D_MODEL

BF16 = jnp.bfloat16
F32 = jnp.float32

VMEM_LIMIT_BYTES = 52 * 1024 * 1024

TM_INPROJ, TN_INPROJ = 1024, 2048
TM_MIX = 256
TM_FFN, TF_FFN, TN_FFN_OUT = 1024, 512, 512
TM_PLE = 512
SB_Q_BLOCK, SB_K_BLOCK = 512, 256
SB_DEAD_LOG_WEIGHT = -110.0
SB_HEADS_PER_STEP = 4
CA_BLOCK = 128
CA_KEY_BLOCKS = LEFT_CHUNKS * CHUNK // CA_BLOCK + 1
CA_GROUP = 4


def _params(*semantics):
    return pltpu.CompilerParams(dimension_semantics=semantics, vmem_limit_bytes=VMEM_LIMIT_BYTES)


def _rmsnorm_f32(x, g):
    ms = jnp.mean(x * x, axis=-1, keepdims=True)
    return x * lax.rsqrt(ms + EPS) * g


def _dot(a, b):
    return jnp.dot(a, b, preferred_element_type=F32)


def _dot_nt(a, b):
    return lax.dot_general(a, b, (((1,), (1,)), ((), ())), preferred_element_type=F32)


def _query_column_scale():
    col = jnp.arange(IN_COLS)[None, :]
    is_query = (col < W_ATT) | ((col >= 3 * W_ATT) & (col < 4 * W_ATT))
    return jnp.where(is_query, HEAD_DIM ** -0.5, 1.0).astype(F32)


def _inproj_kernel(x_ref, g_ref, w_ref, cs_ref, o_ref, h_ref):
    @pl.when(pl.program_id(1) == 0)
    def _():
        h_ref[...] = _rmsnorm_f32(x_ref[...], g_ref[...]).astype(BF16)

    o_ref[...] = (_dot(h_ref[...], w_ref[...]) * cs_ref[...]).astype(o_ref.dtype)


def _in_proj(x2d, g, w_bf16):
    t, d = x2d.shape
    n = w_bf16.shape[1]
    tm, tn = TM_INPROJ, TN_INPROJ
    return pl.pallas_call(
        _inproj_kernel,
        grid=(t // tm, n // tn),
        in_specs=[
            pl.BlockSpec((tm, d), lambda i, j: (i, 0)),
            pl.BlockSpec((1, d), lambda i, j: (0, 0)),
            pl.BlockSpec((d, tn), lambda i, j: (0, j)),
            pl.BlockSpec((1, tn), lambda i, j: (0, j)),
        ],
        out_specs=pl.BlockSpec((tm, tn), lambda i, j: (i, j)),
        out_shape=jax.ShapeDtypeStruct((t, n), BF16),
        scratch_shapes=[pltpu.VMEM((tm, d), BF16)],
        compiler_params=_params("parallel", "arbitrary"),
        name="in_proj",
    )(x2d, g, w_bf16, _query_column_scale())


def _sb_kernel(q_ref, k_ref, v_ref, o_ref):
    tq, tk = SB_Q_BLOCK, SB_K_BLOCK
    assert tq == 2 * tk
    heads = range(SB_HEADS_PER_STEP)
    i = pl.program_id(2)

    def cols(h):
        return slice(h * HEAD_DIM, (h + 1) * HEAD_DIM)

    row = lax.broadcasted_iota(jnp.int32, (2 * tk, tk), 0)
    col = lax.broadcasted_iota(jnp.int32, (2 * tk, tk), 1)
    suffix2 = jnp.where((row & (tk - 1)) > col, 1.0, 0.0).astype(BF16)

    def block(qs, j, accs, carries, diagonal):
        rows = qs[0].shape[0]
        start = pl.multiple_of(j * tk, tk)
        zs = [_dot_nt(q, k_ref[pl.ds(start, tk), cols(h)]) for h, q in zip(heads, qs)]
        if diagonal:
            past = (lax.broadcasted_iota(jnp.int32, (rows, tk), 1)
                    < lax.broadcasted_iota(jnp.int32, (rows, tk), 0))
        log_betas, log_keeps, pieces = [], [], []
        for z in zs:
            log_beta = jnp.minimum(z, 0.0) - jnp.log(1.0 + jnp.exp(-jnp.abs(z)))
            log_keep = log_beta - z
            if diagonal:
                log_keep = jnp.where(past, log_keep, 0.0)
            hi = log_keep.astype(BF16)
            lo = (log_keep - hi.astype(F32)).astype(BF16)
            log_betas.append(log_beta)
            log_keeps.append(log_keep)
            pieces.append(jnp.concatenate([hi, lo], axis=1))
        sums = [_dot(p, suffix2) for p in pieces]
        weights = []
        for log_beta, between, carry in zip(log_betas, sums, carries):
            a = jnp.exp(log_beta + between + carry)
            if diagonal:
                a = jnp.where(past, a, 0.0)
            weights.append(a.astype(BF16))
        accs = [acc + _dot(a, v_ref[pl.ds(start, tk), cols(h)]) for h, a, acc in zip(heads, weights, accs)]
        carries = [c + jnp.sum(lk, axis=1, keepdims=True) for c, lk in zip(carries, log_keeps)]
        return accs, carries

    accs, carries = block([q_ref[tk:, cols(h)] for h in heads], 2 * i + 1,
                          [jnp.zeros((tk, HEAD_DIM), F32) for _ in heads],
                          [jnp.zeros((tk, 1), F32) for _ in heads], True)
    accs = [jnp.concatenate([jnp.zeros((tk, HEAD_DIM), F32), a], axis=0) for a in accs]
    carries = [jnp.concatenate([jnp.zeros((tk, 1), F32), c], axis=0) for c in carries]
    qs = [q_ref[:, cols(h)] for h in heads]
    accs, carries = block(qs, 2 * i, accs, carries, True)

    def alive(carries):
        return jnp.max(functools.reduce(jnp.maximum, carries)) > SB_DEAD_LOG_WEIGHT

    def cond(state):
        j, live, _, _ = state
        return jnp.logical_and(j >= 0, live)

    def body(state):
        j, _, accs, carries = state
        accs, carries = block(qs, j, accs, carries, False)
        return j - 1, alive(carries), accs, carries

    _, _, accs, _ = lax.while_loop(cond, body, (2 * i - 1, alive(carries), accs, carries))
    for h, acc in zip(heads, accs):
        o_ref[:, cols(h)] = acc.astype(o_ref.dtype)


def _sb_attention(proj3d):
    b, s, _ = proj3d.shape
    blk = SB_Q_BLOCK
    width = SB_HEADS_PER_STEP * HEAD_DIM
    groups = W_ATT // width
    return pl.pallas_call(
        _sb_kernel,
        grid=(b, groups, s // blk),
        in_specs=[
            pl.BlockSpec((None, blk, width), lambda bi, g, i: (bi, i, g)),
            pl.BlockSpec((None, s, width), lambda bi, g, i: (bi, 0, groups + g)),
            pl.BlockSpec((None, s, width), lambda bi, g, i: (bi, 0, 2 * groups + g)),
        ],
        out_specs=pl.BlockSpec((None, blk, width), lambda bi, g, i: (bi, i, g)),
        out_shape=jax.ShapeDtypeStruct((b, s, W_ATT), BF16),
        compiler_params=_params("parallel", "parallel", "arbitrary"),
        name="sb_attention",
    )(proj3d, proj3d, proj3d)


def _ca_bias_table(rel_bias):
    blk, width = CA_BLOCK, CA_KEY_BLOCKS * CA_BLOCK
    left = LEFT_CHUNKS * CHUNK
    h = rel_bias.shape[0]
    rb = rel_bias.astype(F32)
    n_lo = left + blk - 1 - REL_CLIP
    n_hi = blk - CHUNK + 1
    f = jnp.concatenate([jnp.broadcast_to(rb[:, :1], (h, n_lo)), rb,
                         jnp.broadcast_to(rb[:, -1:], (h, n_hi))], axis=1)
    n = f.shape[1]
    skew = jnp.tile(f, (1, blk))[:, :blk * (n - 1)].reshape(h, blk, n - 1)
    toeplitz = skew[:, :, blk - 1:blk - 1 + width]
    t = jnp.arange(blk)[:, None]
    c = jnp.arange(width)[None, :]
    dchunk = c // CHUNK - LEFT_CHUNKS - t // CHUNK
    valid = (dchunk >= -LEFT_CHUNKS) & (dchunk <= 0)
    return jnp.where(valid[None], toeplitz, NEG)


def _ca_kernel(q_ref, k_ref, v_ref, t_ref, o_ref):
    blk, nb = CA_BLOCK, CA_KEY_BLOCKS

    def rows(i):
        return slice(i * blk, (i + 1) * blk)

    def keys(i):
        return slice(max(0, i - (nb - 1)) * blk, (i + 1) * blk)

    def scores(i):
        nk = i - max(0, i - (nb - 1)) + 1
        return _dot_nt(q_ref[rows(i), :], k_ref[keys(i), :]) + t_ref[:, (nb - nk) * blk:]

    n_blocks = q_ref.shape[0] // blk
    for g in range(0, n_blocks, CA_GROUP):
        group = range(g, g + CA_GROUP)
        zs = [scores(i) for i in group]
        ms = [jnp.max(z, axis=1, keepdims=True) for z in zs]
        es = [jnp.exp(z - m) for z, m in zip(zs, ms)]
        denoms = [jnp.sum(e, axis=1, keepdims=True) for e in es]
        outs = [_dot(e.astype(BF16), v_ref[keys(i), :]) for e, i in zip(es, group)]
        for i, out, denom in zip(group, outs, denoms):
            o_ref[rows(i), :] = (out / denom).astype(o_ref.dtype)


def _ca_attention(proj3d, table):
    b, s, _ = proj3d.shape
    hb = W_ATT // HEAD_DIM

    def seq_spec(col0):
        return pl.BlockSpec((None, s, HEAD_DIM), lambda bi, h: (bi, 0, col0 + h))

    return pl.pallas_call(
        _ca_kernel,
        grid=(b, N_HEADS),
        in_specs=[
            seq_spec(3 * hb),
            seq_spec(4 * hb),
            seq_spec(5 * hb),
            pl.BlockSpec((None,) + table.shape[1:], lambda bi, h: (h, 0, 0)),
        ],
        out_specs=pl.BlockSpec((None, s, HEAD_DIM), lambda bi, h: (bi, 0, h)),
        out_shape=jax.ShapeDtypeStruct((b, s, W_ATT), BF16),
        compiler_params=_params("parallel", "parallel"),
        name="ca_attention",
    )(proj3d, proj3d, proj3d, table)


def _resident(shape):
    return pl.BlockSpec(shape, lambda *_: (0,) * len(shape), pipeline_mode=pl.Buffered(1))


def _mix_kernel(ysb_ref, yca_ref, gsb_ref, gca_ref, x_ref, wsb_ref, wca_ref, wmix_ref, g_ref,
                x1_ref, h_ref):
    sb = _dot(ysb_ref[...], wsb_ref[...])
    ca = _dot(yca_ref[...], wca_ref[...])
    merged = (jax.nn.sigmoid(gsb_ref[...].astype(F32)) * sb
              + jax.nn.sigmoid(gca_ref[...].astype(F32)) * ca)
    x1 = x_ref[...] + _dot(merged.astype(BF16), wmix_ref[...])
    x1_ref[...] = x1
    h_ref[...] = _rmsnorm_f32(x1, g_ref[...]).astype(BF16)


def _mix_out(y_sb, y_ca, proj2d, x2d, w_sb, w_ca, w_mix, g_ffn):
    t = x2d.shape[0]
    tm = TM_MIX
    gate_blk = 6 * W_ATT // D_MODEL
    att_block = pl.BlockSpec((tm, W_ATT), lambda i: (i, 0))
    row_block = pl.BlockSpec((tm, D_MODEL), lambda i: (i, 0))
    return pl.pallas_call(
        _mix_kernel,
        grid=(t // tm,),
        in_specs=[
            att_block,
            att_block,
            pl.BlockSpec((tm, D_MODEL), lambda i: (i, gate_blk)),
            pl.BlockSpec((tm, D_MODEL), lambda i: (i, gate_blk + 1)),
            row_block,
            _resident((W_ATT, D_MODEL)),
            _resident((W_ATT, D_MODEL)),
            _resident((D_MODEL, D_MODEL)),
            _resident((1, D_MODEL)),
        ],
        out_specs=[row_block, row_block],
        out_shape=[jax.ShapeDtypeStruct((t, D_MODEL), F32), jax.ShapeDtypeStruct((t, D_MODEL), BF16)],
        compiler_params=_params("parallel"),
        name="mix_out",
    )(y_sb, y_ca, proj2d, proj2d, x2d, w_sb, w_ca, w_mix, g_ffn)


def _ffn_kernel(h_ref, wg_ref, wu_ref, wo_ref, o_ref):
    @pl.when(pl.program_id(1) == 0)
    def _():
        o_ref[...] = jnp.zeros_like(o_ref)

    h = h_ref[...]
    gate = _dot(h, wg_ref[...])
    up = _dot(h, wu_ref[...])
    act = (gate * jax.nn.sigmoid(gate) * up).astype(BF16)
    for c in range(0, D_MODEL, TN_FFN_OUT):
        cols = slice(c, c + TN_FFN_OUT)
        o_ref[:, cols] += _dot(act, wo_ref[:, cols])


def _ffn(h, w_in, w_out):
    t = h.shape[0]
    tm, tf = TM_FFN, TF_FFN
    nf = D_FF // tf
    return pl.pallas_call(
        _ffn_kernel,
        grid=(t // tm, nf),
        in_specs=[
            pl.BlockSpec((tm, D_MODEL), lambda i, f: (i, 0)),
            pl.BlockSpec((D_MODEL, tf), lambda i, f: (0, f)),
            pl.BlockSpec((D_MODEL, tf), lambda i, f: (0, nf + f)),
            pl.BlockSpec((tf, D_MODEL), lambda i, f: (f, 0)),
        ],
        out_specs=pl.BlockSpec((tm, D_MODEL), lambda i, f: (i, 0)),
        out_shape=jax.ShapeDtypeStruct((t, D_MODEL), F32),
        compiler_params=_params("parallel", "arbitrary"),
        name="ffn",
    )(h, w_in, w_in, w_out)


def _ple_kernel(x1_ref, y_ref, p_ref, gp_ref, gf_ref, wgate_ref, wple_ref, o_ref):
    x = x1_ref[...] + y_ref[...]
    h = _rmsnorm_f32(x, gp_ref[...]).astype(BF16)
    gate = jax.nn.sigmoid(_dot(h, wgate_ref[...]))
    emb = _dot(p_ref[...].astype(BF16), wple_ref[...])
    o_ref[...] = _rmsnorm_f32(x + gate * emb, gf_ref[...])


def _ple_final(x1, y_ffn, p2d, g_ple, g_final, w_gate, w_ple):
    t = x1.shape[0]
    tm = TM_PLE
    row_block = pl.BlockSpec((tm, D_MODEL), lambda i: (i, 0))
    return pl.pallas_call(
        _ple_kernel,
        grid=(t // tm,),
        in_specs=[
            row_block,
            row_block,
            pl.BlockSpec((tm, D_PLE), lambda i: (i, 0)),
            _resident((1, D_MODEL)),
            _resident((1, D_MODEL)),
            _resident((D_MODEL, D_MODEL)),
            _resident((D_PLE, D_MODEL)),
        ],
        out_specs=row_block,
        out_shape=jax.ShapeDtypeStruct((t, D_MODEL), F32),
        compiler_params=_params("parallel"),
        name="ple_final",
    )(x1, y_ffn, p2d, g_ple, g_final, w_gate, w_ple)


def kernel(x, p, w_in, w_sb_out, w_ca_out, w_mix_out, rel_bias, g_mix, g_ffn, g_ple, g_final,
           w_ffn_in, w_ffn_out, w_ple_in, w_ple_gate):
    b, s, d = x.shape
    assert w_in.shape[0] == 1, "the output norm is fused into the (single) layer's last kernel"
    xt = x.reshape(b * s, d)
    proj = _in_proj(xt, g_mix[0][None], w_in[0].astype(BF16))
    proj3d = proj.reshape(b, s, IN_COLS)
    y_sb = _sb_attention(proj3d).reshape(b * s, W_ATT)
    y_ca = _ca_attention(proj3d, _ca_bias_table(rel_bias[0])).reshape(b * s, W_ATT)
    x1, h_ffn = _mix_out(y_sb, y_ca, proj, xt, w_sb_out[0].astype(BF16), w_ca_out[0].astype(BF16),
                         w_mix_out[0].astype(BF16), g_ffn[0][None])
    y_ffn = _ffn(h_ffn, w_ffn_in[0].astype(BF16), w_ffn_out[0].astype(BF16))
    out = _ple_final(x1, y_ffn, p[0].reshape(b * s, D_PLE), g_ple[0][None], g_final[None],
                     w_ple_gate[0].astype(BF16), w_ple_in[0].astype(BF16))
    return out.reshape(b, s, d)
```

```python
import functools

import jax
import jax.numpy as jnp
from jax import lax
from jax.experimental import pallas as pl
from jax.experimental.pallas import tpu as pltpu

D_MODEL = 2048
CHUNK = 64
HEAD_DIM = 128
N_HEADS = 8
W_ATT = N_HEADS * HEAD_DIM
LEFT_CHUNKS = 8
REL_CLIP = 128
N_REL = REL_CLIP + CHUNK
D_FF = 5632
D_PLE = 256
EPS = 1e-6
NEG = -1e30
IN_COLS = 6 * W_ATT + 2 * D_MODEL

BF16 = jnp.bfloat16
F32 = jnp.float32

LANES = 128
BF16_SUBLANES = 16

VMEM_LIMIT_BYTES = 52 * 1024 * 1024

TM_INPROJ, TN_INPROJ = 1024, 2048
TM_MIX = 256
TM_FFN, TF_FFN, TN_FFN_OUT = 1024, 512, 512
TM_PLE = 512
SB_Q_BLOCK, SB_K_BLOCK = 512, 256
SB_DEAD_LOG_WEIGHT = -110.0
SB_HEADS_PER_STEP = 4
CA_BLOCK = 128
CA_KEY_BLOCKS = LEFT_CHUNKS * CHUNK // CA_BLOCK + 1
CA_GROUP = 4


def _params(*semantics):
    return pltpu.CompilerParams(dimension_semantics=semantics, vmem_limit_bytes=VMEM_LIMIT_BYTES)


def _rmsnorm_f32(x, g):
    ms = jnp.mean(x * x, axis=-1, keepdims=True)
    return x * lax.rsqrt(ms + EPS) * g


def _dot(a, b):
    return jnp.dot(a, b, preferred_element_type=F32)


def _dot_nt(a, b):
    return lax.dot_general(a, b, (((1,), (1,)), ((), ())), preferred_element_type=F32)


def _query_column_scale():
    col = jnp.arange(IN_COLS)[None, :]
    is_query = (col < W_ATT) | ((col >= 3 * W_ATT) & (col < 4 * W_ATT))
    return jnp.where(is_query, HEAD_DIM ** -0.5, 1.0).astype(F32)


def _inproj_kernel(x_ref, g_ref, w_ref, cs_ref, o_ref, h_ref):
    @pl.when(pl.program_id(1) == 0)
    def _():
        h_ref[...] = _rmsnorm_f32(x_ref[...], g_ref[...]).astype(BF16)

    o_ref[...] = (_dot(h_ref[...], w_ref[...]) * cs_ref[...]).astype(o_ref.dtype)


def _in_proj(x2d, g, w_bf16):
    t, d = x2d.shape
    n = w_bf16.shape[1]
    tm, tn = TM_INPROJ, TN_INPROJ
    return pl.pallas_call(
        _inproj_kernel,
        grid=(t // tm, n // tn),
        in_specs=[
            pl.BlockSpec((tm, d), lambda i, j: (i, 0)),
            pl.BlockSpec((1, d), lambda i, j: (0, 0)),
            pl.BlockSpec((d, tn), lambda i, j: (0, j)),
            pl.BlockSpec((1, tn), lambda i, j: (0, j)),
        ],
        out_specs=pl.BlockSpec((tm, tn), lambda i, j: (i, j)),
        out_shape=jax.ShapeDtypeStruct((t, n), BF16),
        scratch_shapes=[pltpu.VMEM((tm, d), BF16)],
        compiler_params=_params("parallel", "arbitrary"),
        name="in_proj",
    )(x2d, g, w_bf16, _query_column_scale())


def _sb_kernel(q_ref, k_ref, v_ref, *refs):
    n_cast = len(refs) // 2
    o_ref = refs[n_cast]
    for src_ref, dst_ref in zip(refs[:n_cast], refs[n_cast + 1:]):
        dst_ref[...] = src_ref[...].astype(dst_ref.dtype)

    tq, tk = SB_Q_BLOCK, SB_K_BLOCK
    assert tq == 2 * tk
    heads = range(SB_HEADS_PER_STEP)
    i = pl.program_id(2)

    def cols(h):
        return slice(h * HEAD_DIM, (h + 1) * HEAD_DIM)

    row = lax.broadcasted_iota(jnp.int32, (tk, tk), 0)
    col = lax.broadcasted_iota(jnp.int32, (tk, tk), 1)
    suffix = jnp.where(row > col, 1.0, 0.0).astype(BF16)

    def block(qs, j, accs, carries, diagonal):
        rows = qs[0].shape[0]
        start = pl.multiple_of(j * tk, tk)
        zs = [_dot_nt(q, k_ref[pl.ds(start, tk), cols(h)]) for h, q in zip(heads, qs)]
        if diagonal:
            past = (lax.broadcasted_iota(jnp.int32, (rows, tk), 1)
                    < lax.broadcasted_iota(jnp.int32, (rows, tk), 0))
        log_betas, log_keeps = [], []
        for z in zs:
            log_beta = jnp.minimum(z, 0.0) - jnp.log(1.0 + jnp.exp(-jnp.abs(z)))
            log_keep = log_beta - z
            if diagonal:
                log_keep = jnp.where(past, log_keep, 0.0)
            log_betas.append(log_beta)
            log_keeps.append(log_keep)
        sums = [_dot(lk.astype(BF16), suffix) for lk in log_keeps]
        weights = []
        for log_beta, between, carry in zip(log_betas, sums, carries):
            a = jnp.exp(log_beta + between + carry)
            if diagonal:
                a = jnp.where(past, a, 0.0)
            weights.append(a.astype(BF16))
        accs = [acc + _dot(a, v_ref[pl.ds(start, tk), cols(h)]) for h, a, acc in zip(heads, weights, accs)]
        carries = [c + jnp.sum(lk, axis=1, keepdims=True) for c, lk in zip(carries, log_keeps)]
        return accs, carries

    accs, carries = block([q_ref[tk:, cols(h)] for h in heads], 2 * i + 1,
                          [jnp.zeros((tk, HEAD_DIM), F32) for _ in heads],
                          [jnp.zeros((tk, 1), F32) for _ in heads], True)
    accs = [jnp.concatenate([jnp.zeros((tk, HEAD_DIM), F32), a], axis=0) for a in accs]
    carries = [jnp.concatenate([jnp.zeros((tk, 1), F32), c], axis=0) for c in carries]
    qs = [q_ref[:, cols(h)] for h in heads]
    accs, carries = block(qs, 2 * i, accs, carries, True)

    def alive(carries):
        return jnp.max(functools.reduce(jnp.maximum, carries)) > SB_DEAD_LOG_WEIGHT

    def cond(state):
        j, live, _, _ = state
        return jnp.logical_and(j >= 0, live)

    def body(state):
        j, _, accs, carries = state
        accs, carries = block(qs, j, accs, carries, False)
        return j - 1, alive(carries), accs, carries

    _, _, accs, _ = lax.while_loop(cond, body, (2 * i - 1, alive(carries), accs, carries))
    for h, acc in zip(heads, accs):
        o_ref[:, cols(h)] = acc.astype(o_ref.dtype)


def _row_chunked(w, n_chunks):
    rows, cols = w.shape
    fold = next(f for f in (1, 2, 4, 8) if (rows * f) % (n_chunks * BF16_SUBLANES) == 0
                and cols % (f * LANES) == 0)
    return w.reshape(rows * fold, cols // fold)


def _sb_attention(proj3d, weights):
    b, s, _ = proj3d.shape
    blk = SB_Q_BLOCK
    width = SB_HEADS_PER_STEP * HEAD_DIM
    groups = W_ATT // width
    q_blocks = s // blk
    n_steps = b * groups * q_blocks
    chunked = [_row_chunked(w, n_steps) for w in weights]
    cast_specs = [pl.BlockSpec((w.shape[0] // n_steps, w.shape[1]),
                               lambda bi, g, i: ((bi * groups + g) * q_blocks + i, 0)) for w in chunked]
    y_spec = pl.BlockSpec((None, blk, width), lambda bi, g, i: (bi, i, g))
    y, *cast = pl.pallas_call(
        _sb_kernel,
        grid=(b, groups, q_blocks),
        in_specs=[
            y_spec,
            pl.BlockSpec((None, s, width), lambda bi, g, i: (bi, 0, groups + g)),
            pl.BlockSpec((None, s, width), lambda bi, g, i: (bi, 0, 2 * groups + g)),
        ] + cast_specs,
        out_specs=[y_spec] + cast_specs,
        out_shape=[jax.ShapeDtypeStruct((b, s, W_ATT), BF16)]
        + [jax.ShapeDtypeStruct(w.shape, BF16) for w in chunked],
        compiler_params=_params("parallel", "parallel", "arbitrary"),
        name="sb_attention",
    )(proj3d, proj3d, proj3d, *chunked)
    return y, [c.reshape(w.shape) for c, w in zip(cast, weights)]


def _ca_bias_table(rel_bias):
    blk, width = CA_BLOCK, CA_KEY_BLOCKS * CA_BLOCK
    left = LEFT_CHUNKS * CHUNK
    h = rel_bias.shape[0]
    rb = rel_bias.astype(F32)
    n_lo = left + blk - 1 - REL_CLIP
    n_hi = blk - CHUNK + 1
    f = jnp.concatenate([jnp.broadcast_to(rb[:, :1], (h, n_lo)), rb,
                         jnp.broadcast_to(rb[:, -1:], (h, n_hi))], axis=1)
    n = f.shape[1]
    skew = jnp.tile(f, (1, blk))[:, :blk * (n - 1)].reshape(h, blk, n - 1)
    toeplitz = skew[:, :, blk - 1:blk - 1 + width]
    t = jnp.arange(blk)[:, None]
    c = jnp.arange(width)[None, :]
    dchunk = c // CHUNK - LEFT_CHUNKS - t // CHUNK
    valid = (dchunk >= -LEFT_CHUNKS) & (dchunk <= 0)
    return jnp.where(valid[None], toeplitz, NEG)


def _ca_kernel(q_ref, k_ref, v_ref, t_ref, o_ref):
    blk, nb = CA_BLOCK, CA_KEY_BLOCKS

    def rows(i):
        return slice(i * blk, (i + 1) * blk)

    def keys(i):
        return slice(max(0, i - (nb - 1)) * blk, (i + 1) * blk)

    def scores(i):
        nk = i - max(0, i - (nb - 1)) + 1
        return _dot_nt(q_ref[rows(i), :], k_ref[keys(i), :]) + t_ref[:, (nb - nk) * blk:]

    n_blocks = q_ref.shape[0] // blk
    for g in range(0, n_blocks, CA_GROUP):
        group = range(g, g + CA_GROUP)
        zs = [scores(i) for i in group]
        ms = [jnp.max(z, axis=1, keepdims=True) for z in zs]
        es = [jnp.exp(z - m) for z, m in zip(zs, ms)]
        denoms = [jnp.sum(e, axis=1, keepdims=True) for e in es]
        outs = [_dot(e.astype(BF16), v_ref[keys(i), :]) for e, i in zip(es, group)]
        for i, out, denom in zip(group, outs, denoms):
            o_ref[rows(i), :] = (out / denom).astype(o_ref.dtype)


def _ca_attention(proj3d, table):
    b, s, _ = proj3d.shape
    hb = W_ATT // HEAD_DIM

    def seq_spec(col0):
        return pl.BlockSpec((None, s, HEAD_DIM), lambda bi, h: (bi, 0, col0 + h))

    return pl.pallas_call(
        _ca_kernel,
        grid=(b, N_HEADS),
        in_specs=[
            seq_spec(3 * hb),
            seq_spec(4 * hb),
            seq_spec(5 * hb),
            pl.BlockSpec((None,) + table.shape[1:], lambda bi, h: (h, 0, 0)),
        ],
        out_specs=pl.BlockSpec((None, s, HEAD_DIM), lambda bi, h: (bi, 0, h)),
        out_shape=jax.ShapeDtypeStruct((b, s, W_ATT), BF16),
        compiler_params=_params("parallel", "parallel"),
        name="ca_attention",
    )(proj3d, proj3d, proj3d, table)


def _resident(shape):
    return pl.BlockSpec(shape, lambda *_: (0,) * len(shape), pipeline_mode=pl.Buffered(1))


def _mix_kernel(ysb_ref, yca_ref, gsb_ref, gca_ref, x_ref, wsb_ref, wca_ref, wmix_ref, g_ref,
                x1_ref, h_ref):
    sb = _dot(ysb_ref[...], wsb_ref[...])
    ca = _dot(yca_ref[...], wca_ref[...])
    merged = (jax.nn.sigmoid(gsb_ref[...].astype(F32)) * sb
              + jax.nn.sigmoid(gca_ref[...].astype(F32)) * ca)
    x1 = x_ref[...] + _dot(merged.astype(BF16), wmix_ref[...])
    x1_ref[...] = x1
    h_ref[...] = _rmsnorm_f32(x1, g_ref[...]).astype(BF16)


def _mix_out(y_sb, y_ca, proj2d, x2d, w_sb, w_ca, w_mix, g_ffn):
    t = x2d.shape[0]
    tm = TM_MIX
    gate_blk = 6 * W_ATT // D_MODEL
    att_block = pl.BlockSpec((tm, W_ATT), lambda i: (i, 0))
    row_block = pl.BlockSpec((tm, D_MODEL), lambda i: (i, 0))
    return pl.pallas_call(
        _mix_kernel,
        grid=(t // tm,),
        in_specs=[
            att_block,
            att_block,
            pl.BlockSpec((tm, D_MODEL), lambda i: (i, gate_blk)),
            pl.BlockSpec((tm, D_MODEL), lambda i: (i, gate_blk + 1)),
            row_block,
            _resident((W_ATT, D_MODEL)),
            _resident((W_ATT, D_MODEL)),
            _resident((D_MODEL, D_MODEL)),
            _resident((1, D_MODEL)),
        ],
        out_specs=[row_block, row_block],
        out_shape=[jax.ShapeDtypeStruct((t, D_MODEL), F32), jax.ShapeDtypeStruct((t, D_MODEL), BF16)],
        compiler_params=_params("parallel"),
        name="mix_out",
    )(y_sb, y_ca, proj2d, proj2d, x2d, w_sb, w_ca, w_mix, g_ffn)


def _ffn_kernel(h_ref, wg_ref, wu_ref, wo_ref, o_ref):
    @pl.when(pl.program_id(1) == 0)
    def _():
        o_ref[...] = jnp.zeros_like(o_ref)

    h = h_ref[...]
    gate = _dot(h, wg_ref[...])
    up = _dot(h, wu_ref[...])
    act = (gate * jax.nn.sigmoid(gate) * up).astype(BF16)
    for c in range(0, D_MODEL, TN_FFN_OUT):
        cols = slice(c, c + TN_FFN_OUT)
        o_ref[:, cols] += _dot(act, wo_ref[:, cols])


def _ffn(h, w_in, w_out):
    t = h.shape[0]
    tm, tf = TM_FFN, TF_FFN
    nf = D_FF // tf
    return pl.pallas_call(
        _ffn_kernel,
        grid=(t // tm, nf),
        in_specs=[
            pl.BlockSpec((tm, D_MODEL), lambda i, f: (i, 0)),
            pl.BlockSpec((D_MODEL, tf), lambda i, f: (0, f)),
            pl.BlockSpec((D_MODEL, tf), lambda i, f: (0, nf + f)),
            pl.BlockSpec((tf, D_MODEL), lambda i, f: (f, 0)),
        ],
        out_specs=pl.BlockSpec((tm, D_MODEL), lambda i, f: (i, 0)),
        out_shape=jax.ShapeDtypeStruct((t, D_MODEL), F32),
        compiler_params=_params("parallel", "arbitrary"),
        name="ffn",
    )(h, w_in, w_in, w_out)


def _ple_kernel(x1_ref, y_ref, p_ref, gp_ref, gf_ref, wgate_ref, wple_ref, o_ref):
    x = x1_ref[...] + y_ref[...]
    h = _rmsnorm_f32(x, gp_ref[...]).astype(BF16)
    gate = jax.nn.sigmoid(_dot(h, wgate_ref[...]))
    emb = _dot(p_ref[...].astype(BF16), wple_ref[...])
    o_ref[...] = _rmsnorm_f32(x + gate * emb, gf_ref[...])


def _ple_final(x1, y_ffn, p2d, g_ple, g_final, w_gate, w_ple):
    t = x1.shape[0]
    tm = TM_PLE
    row_block = pl.BlockSpec((tm, D_MODEL), lambda i: (i, 0))
    return pl.pallas_call(
        _ple_kernel,
        grid=(t // tm,),
        in_specs=[
            row_block,
            row_block,
            pl.BlockSpec((tm, D_PLE), lambda i: (i, 0)),
            _resident((1, D_MODEL)),
            _resident((1, D_MODEL)),
            _resident((D_MODEL, D_MODEL)),
            _resident((D_PLE, D_MODEL)),
        ],
        out_specs=row_block,
        out_shape=jax.ShapeDtypeStruct((t, D_MODEL), F32),
        compiler_params=_params("parallel"),
        name="ple_final",
    )(x1, y_ffn, p2d, g_ple, g_final, w_gate, w_ple)


def kernel(x, p, w_in, w_sb_out, w_ca_out, w_mix_out, rel_bias, g_mix, g_ffn, g_ple, g_final,
           w_ffn_in, w_ffn_out, w_ple_in, w_ple_gate):
    b, s, d = x.shape
    assert w_in.shape[0] == 1, "the output norm is fused into the (single) layer's last kernel"
    xt = x.reshape(b * s, d)
    proj = _in_proj(xt, g_mix[0][None], w_in[0].astype(BF16))
    proj3d = proj.reshape(b, s, IN_COLS)
    y_sb, (w_sb, w_ca, w_mix, w_ffn_a, w_ffn_b, w_gate, w_ple) = _sb_attention(
        proj3d, [w_sb_out[0], w_ca_out[0], w_mix_out[0], w_ffn_in[0], w_ffn_out[0],
                 w_ple_gate[0], w_ple_in[0]])
    y_sb = y_sb.reshape(b * s, W_ATT)
    y_ca = _ca_attention(proj3d, _ca_bias_table(rel_bias[0])).reshape(b * s, W_ATT)
    x1, h_ffn = _mix_out(y_sb, y_ca, proj, xt, w_sb, w_ca, w_mix, g_ffn[0][None])
    y_ffn = _ffn(h_ffn, w_ffn_a, w_ffn_b)
    out = _ple_final(x1, y_ffn, p[0].reshape(b * s, D_PLE), g_ple[0][None], g_final[None],
                     w_gate, w_ple)
    return out.reshape(b, s, d)
```

```python
import functools

import jax
import jax.numpy as jnp
from jax import lax
from jax.experimental import pallas as pl
from jax.experimental.pallas import tpu as pltpu

D_MODEL = 2048
CHUNK = 64
HEAD_DIM = 128
N_HEADS = 8
W_ATT = N_HEADS * HEAD_DIM
LEFT_CHUNKS = 8
REL_CLIP = 128
N_REL = REL_CLIP + CHUNK
D_FF = 5632
D_PLE = 256
EPS = 1e-6
NEG = -1e30
IN_COLS = 6 * W_ATT + 2 * D_MODEL

BF16 = jnp.bfloat16
F32 = jnp.float32

BF16_SUBLANES = 16

VMEM_LIMIT_BYTES = 52 * 1024 * 1024

TM_INPROJ, TN_INPROJ = 1024, 2048
TM_MIX = 256
TM_FFN, TF_FFN, TN_FFN_OUT = 1024, 512, 512
TM_PLE = 512
SB_Q_BLOCK, SB_K_BLOCK = 512, 256
SB_DEAD_LOG_WEIGHT = -110.0
SB_HEADS_PER_STEP = 4
CA_BLOCK = 128
CA_KEY_BLOCKS = LEFT_CHUNKS * CHUNK // CA_BLOCK + 1
CA_GROUP = 4


def _params(*semantics):
    return pltpu.CompilerParams(dimension_semantics=semantics, vmem_limit_bytes=VMEM_LIMIT_BYTES)


def _rmsnorm_f32(x, g):
    ms = jnp.mean(x * x, axis=-1, keepdims=True)
    return x * lax.rsqrt(ms + EPS) * g


def _dot(a, b):
    return jnp.dot(a, b, preferred_element_type=F32)


def _dot_nt(a, b):
    return lax.dot_general(a, b, (((1,), (1,)), ((), ())), preferred_element_type=F32)


def _query_column_scale():
    col = jnp.arange(IN_COLS)[None, :]
    is_query = (col < W_ATT) | ((col >= 3 * W_ATT) & (col < 4 * W_ATT))
    return jnp.where(is_query, HEAD_DIM ** -0.5, 1.0).astype(F32)


def _inproj_kernel(x_ref, g_ref, w_ref, cs_ref, o_ref, h_ref):
    @pl.when(pl.program_id(1) == 0)
    def _():
        h_ref[...] = _rmsnorm_f32(x_ref[...], g_ref[...]).astype(BF16)

    o_ref[...] = (_dot(h_ref[...], w_ref[...]) * cs_ref[...]).astype(o_ref.dtype)


def _in_proj(x2d, g, w_bf16):
    t, d = x2d.shape
    n = w_bf16.shape[1]
    tm, tn = TM_INPROJ, TN_INPROJ
    return pl.pallas_call(
        _inproj_kernel,
        grid=(t // tm, n // tn),
        in_specs=[
            pl.BlockSpec((tm, d), lambda i, j: (i, 0)),
            pl.BlockSpec((1, d), lambda i, j: (0, 0)),
            pl.BlockSpec((d, tn), lambda i, j: (0, j)),
            pl.BlockSpec((1, tn), lambda i, j: (0, j)),
        ],
        out_specs=pl.BlockSpec((tm, tn), lambda i, j: (i, j)),
        out_shape=jax.ShapeDtypeStruct((t, n), BF16),
        scratch_shapes=[pltpu.VMEM((tm, d), BF16)],
        compiler_params=_params("parallel", "arbitrary"),
        name="in_proj",
    )(x2d, g, w_bf16, _query_column_scale())


def _sb_kernel(q_ref, k_ref, v_ref, *refs):
    n_cast = len(refs) // 2
    o_ref = refs[n_cast]
    for src_ref, dst_ref in zip(refs[:n_cast], refs[n_cast + 1:]):
        dst_ref[...] = src_ref[...].astype(dst_ref.dtype)

    tq, tk = SB_Q_BLOCK, SB_K_BLOCK
    assert tq == 2 * tk
    heads = range(SB_HEADS_PER_STEP)
    i = pl.program_id(2)

    def cols(h):
        return slice(h * HEAD_DIM, (h + 1) * HEAD_DIM)

    row = lax.broadcasted_iota(jnp.int32, (tk, tk), 0)
    col = lax.broadcasted_iota(jnp.int32, (tk, tk), 1)
    suffix = jnp.where(row > col, 1.0, 0.0).astype(BF16)

    def block(qs, j, accs, carries, diagonal):
        rows = qs[0].shape[0]
        start = pl.multiple_of(j * tk, tk)
        zs = [_dot_nt(q, k_ref[pl.ds(start, tk), cols(h)]) for h, q in zip(heads, qs)]
        if diagonal:
            past = (lax.broadcasted_iota(jnp.int32, (rows, tk), 1)
                    < lax.broadcasted_iota(jnp.int32, (rows, tk), 0))
        log_betas, log_keeps = [], []
        for z in zs:
            log_beta = jnp.minimum(z, 0.0) - jnp.log(1.0 + jnp.exp(-jnp.abs(z)))
            log_keep = log_beta - z
            if diagonal:
                log_keep = jnp.where(past, log_keep, 0.0)
            log_betas.append(log_beta)
            log_keeps.append(log_keep)
        sums = [_dot(lk.astype(BF16), suffix) for lk in log_keeps]
        weights = []
        for log_beta, between, carry in zip(log_betas, sums, carries):
            a = jnp.exp(log_beta + between + carry)
            if diagonal:
                a = jnp.where(past, a, 0.0)
            weights.append(a.astype(BF16))
        accs = [acc + _dot(a, v_ref[pl.ds(start, tk), cols(h)]) for h, a, acc in zip(heads, weights, accs)]
        carries = [c + jnp.sum(lk, axis=1, keepdims=True) for c, lk in zip(carries, log_keeps)]
        return accs, carries

    accs, carries = block([q_ref[tk:, cols(h)] for h in heads], 2 * i + 1,
                          [jnp.zeros((tk, HEAD_DIM), F32) for _ in heads],
                          [jnp.zeros((tk, 1), F32) for _ in heads], True)
    accs = [jnp.concatenate([jnp.zeros((tk, HEAD_DIM), F32), a], axis=0) for a in accs]
    carries = [jnp.concatenate([jnp.zeros((tk, 1), F32), c], axis=0) for c in carries]
    qs = [q_ref[:, cols(h)] for h in heads]
    accs, carries = block(qs, 2 * i, accs, carries, True)

    def alive(carries):
        return jnp.max(functools.reduce(jnp.maximum, carries)) > SB_DEAD_LOG_WEIGHT

    def cond(state):
        j, live, _, _ = state
        return jnp.logical_and(j >= 0, live)

    def body(state):
        j, _, accs, carries = state
        accs, carries = block(qs, j, accs, carries, False)
        return j - 1, alive(carries), accs, carries

    _, _, accs, _ = lax.while_loop(cond, body, (2 * i - 1, alive(carries), accs, carries))
    for h, acc in zip(heads, accs):
        o_ref[:, cols(h)] = acc.astype(o_ref.dtype)


def _chunk_rows(rows, max_chunks):
    return next(r for r in range(BF16_SUBLANES, rows + 1, BF16_SUBLANES)
                if rows % r == 0 and rows // r <= max_chunks)


def _sb_attention(proj3d, stacked_weights):
    b, s, _ = proj3d.shape
    blk = SB_Q_BLOCK
    width = SB_HEADS_PER_STEP * HEAD_DIM
    groups = W_ATT // width
    q_blocks = s // blk
    n_steps = b * groups * q_blocks

    in_cast, out_cast, cast_shapes = [], [], []
    for w in stacked_weights:
        _, rows, cols = w.shape
        chunk = _chunk_rows(rows, n_steps)
        last = rows // chunk - 1

        def chunk_index(bi, g, i, last=last):
            return jnp.minimum((bi * groups + g) * q_blocks + i, last)

        in_cast.append(pl.BlockSpec((None, chunk, cols), lambda bi, g, i, f=chunk_index: (0, f(bi, g, i), 0)))
        out_cast.append(pl.BlockSpec((chunk, cols), lambda bi, g, i, f=chunk_index: (f(bi, g, i), 0)))
        cast_shapes.append(jax.ShapeDtypeStruct((rows, cols), BF16))

    y_spec = pl.BlockSpec((None, blk, width), lambda bi, g, i: (bi, i, g))
    y, *cast = pl.pallas_call(
        _sb_kernel,
        grid=(b, groups, q_blocks),
        in_specs=[
            y_spec,
            pl.BlockSpec((None, s, width), lambda bi, g, i: (bi, 0, groups + g)),
            pl.BlockSpec((None, s, width), lambda bi, g, i: (bi, 0, 2 * groups + g)),
        ] + in_cast,
        out_specs=[y_spec] + out_cast,
        out_shape=[jax.ShapeDtypeStruct((b, s, W_ATT), BF16)] + cast_shapes,
        compiler_params=_params("arbitrary", "arbitrary", "arbitrary"),
        name="sb_attention",
    )(proj3d, proj3d, proj3d, *stacked_weights)
    return y, cast


def _ca_bias_table(rel_bias):
    blk, width = CA_BLOCK, CA_KEY_BLOCKS * CA_BLOCK
    left = LEFT_CHUNKS * CHUNK
    h = rel_bias.shape[0]
    rb = rel_bias.astype(F32)
    n_lo = left + blk - 1 - REL_CLIP
    n_hi = blk - CHUNK + 1
    f = jnp.concatenate([jnp.broadcast_to(rb[:, :1], (h, n_lo)), rb,
                         jnp.broadcast_to(rb[:, -1:], (h, n_hi))], axis=1)
    n = f.shape[1]
    skew = jnp.tile(f, (1, blk))[:, :blk * (n - 1)].reshape(h, blk, n - 1)
    toeplitz = skew[:, :, blk - 1:blk - 1 + width]
    t = jnp.arange(blk)[:, None]
    c = jnp.arange(width)[None, :]
    dchunk = c // CHUNK - LEFT_CHUNKS - t // CHUNK
    valid = (dchunk >= -LEFT_CHUNKS) & (dchunk <= 0)
    return jnp.where(valid[None], toeplitz, NEG)


def _ca_kernel(q_ref, k_ref, v_ref, t_ref, o_ref):
    blk, nb = CA_BLOCK, CA_KEY_BLOCKS

    def rows(i):
        return slice(i * blk, (i + 1) * blk)

    def keys(i):
        return slice(max(0, i - (nb - 1)) * blk, (i + 1) * blk)

    def scores(i):
        nk = i - max(0, i - (nb - 1)) + 1
        return _dot_nt(q_ref[rows(i), :], k_ref[keys(i), :]) + t_ref[:, (nb - nk) * blk:]

    n_blocks = q_ref.shape[0] // blk
    for g in range(0, n_blocks, CA_GROUP):
        group = range(g, g + CA_GROUP)
        zs = [scores(i) for i in group]
        ms = [jnp.max(z, axis=1, keepdims=True) for z in zs]
        es = [jnp.exp(z - m) for z, m in zip(zs, ms)]
        denoms = [jnp.sum(e, axis=1, keepdims=True) for e in es]
        outs = [_dot(e.astype(BF16), v_ref[keys(i), :]) for e, i in zip(es, group)]
        for i, out, denom in zip(group, outs, denoms):
            o_ref[rows(i), :] = (out / denom).astype(o_ref.dtype)


def _ca_attention(proj3d, table):
    b, s, _ = proj3d.shape
    hb = W_ATT // HEAD_DIM

    def seq_spec(col0):
        return pl.BlockSpec((None, s, HEAD_DIM), lambda bi, h: (bi, 0, col0 + h))

    return pl.pallas_call(
        _ca_kernel,
        grid=(b, N_HEADS),
        in_specs=[
            seq_spec(3 * hb),
            seq_spec(4 * hb),
            seq_spec(5 * hb),
            pl.BlockSpec((None,) + table.shape[1:], lambda bi, h: (h, 0, 0)),
        ],
        out_specs=pl.BlockSpec((None, s, HEAD_DIM), lambda bi, h: (bi, 0, h)),
        out_shape=jax.ShapeDtypeStruct((b, s, W_ATT), BF16),
        compiler_params=_params("parallel", "parallel"),
        name="ca_attention",
    )(proj3d, proj3d, proj3d, table)


def _resident(shape):
    return pl.BlockSpec(shape, lambda *_: (0,) * len(shape), pipeline_mode=pl.Buffered(1))


def _mix_kernel(ysb_ref, yca_ref, gsb_ref, gca_ref, x_ref, wsb_ref, wca_ref, wmix_ref, g_ref,
                x1_ref, h_ref):
    sb = _dot(ysb_ref[...], wsb_ref[...])
    ca = _dot(yca_ref[...], wca_ref[...])
    merged = (jax.nn.sigmoid(gsb_ref[...].astype(F32)) * sb
              + jax.nn.sigmoid(gca_ref[...].astype(F32)) * ca)
    x1 = x_ref[...] + _dot(merged.astype(BF16), wmix_ref[...])
    x1_ref[...] = x1
    h_ref[...] = _rmsnorm_f32(x1, g_ref[...]).astype(BF16)


def _mix_out(y_sb, y_ca, proj2d, x2d, w_sb, w_ca, w_mix, g_ffn):
    t = x2d.shape[0]
    tm = TM_MIX
    gate_blk = 6 * W_ATT // D_MODEL
    att_block = pl.BlockSpec((tm, W_ATT), lambda i: (i, 0))
    row_block = pl.BlockSpec((tm, D_MODEL), lambda i: (i, 0))
    return pl.pallas_call(
        _mix_kernel,
        grid=(t // tm,),
        in_specs=[
            att_block,
            att_block,
            pl.BlockSpec((tm, D_MODEL), lambda i: (i, gate_blk)),
            pl.BlockSpec((tm, D_MODEL), lambda i: (i, gate_blk + 1)),
            row_block,
            _resident((W_ATT, D_MODEL)),
            _resident((W_ATT, D_MODEL)),
            _resident((D_MODEL, D_MODEL)),
            _resident((1, D_MODEL)),
        ],
        out_specs=[row_block, row_block],
        out_shape=[jax.ShapeDtypeStruct((t, D_MODEL), F32), jax.ShapeDtypeStruct((t, D_MODEL), BF16)],
        compiler_params=_params("parallel"),
        name="mix_out",
    )(y_sb, y_ca, proj2d, proj2d, x2d, w_sb, w_ca, w_mix, g_ffn)


def _ffn_kernel(h_ref, wg_ref, wu_ref, wo_ref, o_ref):
    @pl.when(pl.program_id(1) == 0)
    def _():
        o_ref[...] = jnp.zeros_like(o_ref)

    h = h_ref[...]
    gate = _dot(h, wg_ref[...])
    up = _dot(h, wu_ref[...])
    act = (gate * jax.nn.sigmoid(gate) * up).astype(BF16)
    for c in range(0, D_MODEL, TN_FFN_OUT):
        cols = slice(c, c + TN_FFN_OUT)
        o_ref[:, cols] += _dot(act, wo_ref[:, cols])


def _ffn(h, w_in, w_out):
    t = h.shape[0]
    tm, tf = TM_FFN, TF_FFN
    nf = D_FF // tf
    return pl.pallas_call(
        _ffn_kernel,
        grid=(t // tm, nf),
        in_specs=[
            pl.BlockSpec((tm, D_MODEL), lambda i, f: (i, 0)),
            pl.BlockSpec((D_MODEL, tf), lambda i, f: (0, f)),
            pl.BlockSpec((D_MODEL, tf), lambda i, f: (0, nf + f)),
            pl.BlockSpec((tf, D_MODEL), lambda i, f: (f, 0)),
        ],
        out_specs=pl.BlockSpec((tm, D_MODEL), lambda i, f: (i, 0)),
        out_shape=jax.ShapeDtypeStruct((t, D_MODEL), F32),
        compiler_params=_params("parallel", "arbitrary"),
        name="ffn",
    )(h, w_in, w_in, w_out)


def _ple_kernel(x1_ref, y_ref, p_ref, gp_ref, gf_ref, wgate_ref, wple_ref, o_ref):
    x = x1_ref[...] + y_ref[...]
    h = _rmsnorm_f32(x, gp_ref[...]).astype(BF16)
    gate = jax.nn.sigmoid(_dot(h, wgate_ref[...]))
    emb = _dot(p_ref[...].astype(BF16), wple_ref[...])
    o_ref[...] = _rmsnorm_f32(x + gate * emb, gf_ref[...])


def _ple_final(x1, y_ffn, p_stacked, g_ple, g_final, w_gate, w_ple):
    t = x1.shape[0]
    tm = TM_PLE
    blocks_per_seq = p_stacked.shape[2] // tm
    row_block = pl.BlockSpec((tm, D_MODEL), lambda i: (i, 0))
    return pl.pallas_call(
        _ple_kernel,
        grid=(t // tm,),
        in_specs=[
            row_block,
            row_block,
            pl.BlockSpec((None, None, tm, D_PLE),
                         lambda i: (0, i // blocks_per_seq, i % blocks_per_seq, 0)),
            _resident((1, D_MODEL)),
            _resident((1, D_MODEL)),
            _resident((D_MODEL, D_MODEL)),
            _resident((D_PLE, D_MODEL)),
        ],
        out_specs=row_block,
        out_shape=jax.ShapeDtypeStruct((t, D_MODEL), F32),
        compiler_params=_params("parallel"),
        name="ple_final",
    )(x1, y_ffn, p_stacked, g_ple, g_final, w_gate, w_ple)


def kernel(x, p, w_in, w_sb_out, w_ca_out, w_mix_out, rel_bias, g_mix, g_ffn, g_ple, g_final,
           w_ffn_in, w_ffn_out, w_ple_in, w_ple_gate):
    b, s, d = x.shape
    assert w_in.shape[0] == 1, "the output norm is fused into the (single) layer's last kernel"
    xt = x.reshape(b * s, d)
    proj = _in_proj(xt, g_mix[0][None], w_in[0].astype(BF16))
    proj3d = proj.reshape(b, s, IN_COLS)
    y_sb, (w_sb, w_ca, w_mix, w_ffn_a, w_ffn_b, w_gate, w_ple) = _sb_attention(
        proj3d, [w_sb_out, w_ca_out, w_mix_out, w_ffn_in, w_ffn_out, w_ple_gate, w_ple_in])
    y_sb = y_sb.reshape(b * s, W_ATT)
    y_ca = _ca_attention(proj3d, _ca_bias_table(rel_bias[0])).reshape(b * s, W_ATT)
    x1, h_ffn = _mix_out(y_sb, y_ca, proj, xt, w_sb, w_ca, w_mix, g_ffn[0][None])
    y_ffn = _ffn(h_ffn, w_ffn_a, w_ffn_b)
    out = _ple_final(x1, y_ffn, p, g_ple[0][None], g_final[None],
                     w_gate, w_ple)
    return out.reshape(b, s, d)
```

```python
import functools

import jax
import jax.numpy as jnp
from jax import lax
from jax.experimental import pallas as pl
from jax.experimental.pallas import tpu as pltpu

D_MODEL = 2048
CHUNK = 64
HEAD_DIM = 128
N_HEADS = 8
W_ATT = N_HEADS * HEAD_DIM
LEFT_CHUNKS = 8
REL_CLIP = 128
N_REL = REL_CLIP + CHUNK
D_FF = 5632
D_PLE = 256
EPS = 1e-6
NEG = -1e30
IN_COLS = 6 * W_ATT + 2 * D_MODEL

BF16 = jnp.bfloat16
F32 = jnp.float32

BF16_SUBLANES = 16

VMEM_LIMIT_BYTES = 52 * 1024 * 1024

TM_INPROJ, TN_INPROJ = 1024, 2048
TM_MIX = 256
TM_FFN, TF_FFN, TN_FFN_OUT = 1024, 512, 512
TM_PLE = 512
SB_Q_BLOCK, SB_K_BLOCK = 512, 256
SB_DEAD_LOG_WEIGHT = -110.0
SB_HEADS_PER_STEP = 4
CA_BLOCK = 128
CA_KEY_BLOCKS = LEFT_CHUNKS * CHUNK // CA_BLOCK + 1
CA_GROUP = 2
CA_HEADS_PER_STEP = 2


def _params(*semantics):
    return pltpu.CompilerParams(dimension_semantics=semantics, vmem_limit_bytes=VMEM_LIMIT_BYTES)


def _rmsnorm_f32(x, g):
    ms = jnp.mean(x * x, axis=-1, keepdims=True)
    return x * lax.rsqrt(ms + EPS) * g


def _dot(a, b):
    return jnp.dot(a, b, preferred_element_type=F32)


def _dot_nt(a, b):
    return lax.dot_general(a, b, (((1,), (1,)), ((), ())), preferred_element_type=F32)


def _query_column_scale():
    col = jnp.arange(IN_COLS)[None, :]
    is_query = (col < W_ATT) | ((col >= 3 * W_ATT) & (col < 4 * W_ATT))
    return jnp.where(is_query, HEAD_DIM ** -0.5, 1.0).astype(F32)


def _inproj_kernel(x_ref, g_ref, w_ref, cs_ref, o_ref, h_ref):
    @pl.when(pl.program_id(1) == 0)
    def _():
        h_ref[...] = _rmsnorm_f32(x_ref[...], g_ref[...]).astype(BF16)

    o_ref[...] = (_dot(h_ref[...], w_ref[...]) * cs_ref[...]).astype(o_ref.dtype)


def _in_proj(x2d, g, w_bf16):
    t, d = x2d.shape
    n = w_bf16.shape[1]
    tm, tn = TM_INPROJ, TN_INPROJ
    return pl.pallas_call(
        _inproj_kernel,
        grid=(t // tm, n // tn),
        in_specs=[
            pl.BlockSpec((tm, d), lambda i, j: (i, 0)),
            pl.BlockSpec((1, d), lambda i, j: (0, 0)),
            pl.BlockSpec((d, tn), lambda i, j: (0, j)),
            pl.BlockSpec((1, tn), lambda i, j: (0, j)),
        ],
        out_specs=pl.BlockSpec((tm, tn), lambda i, j: (i, j)),
        out_shape=jax.ShapeDtypeStruct((t, n), BF16),
        scratch_shapes=[pltpu.VMEM((tm, d), BF16)],
        compiler_params=_params("parallel", "arbitrary"),
        name="in_proj",
    )(x2d, g, w_bf16, _query_column_scale())


def _sb_kernel(q_ref, k_ref, v_ref, *refs):
    n_cast = len(refs) // 2
    o_ref = refs[n_cast]
    for src_ref, dst_ref in zip(refs[:n_cast], refs[n_cast + 1:]):
        dst_ref[...] = src_ref[...].astype(dst_ref.dtype)

    tq, tk = SB_Q_BLOCK, SB_K_BLOCK
    assert tq == 2 * tk
    heads = range(SB_HEADS_PER_STEP)
    i = pl.program_id(2)

    def cols(h):
        return slice(h * HEAD_DIM, (h + 1) * HEAD_DIM)

    row = lax.broadcasted_iota(jnp.int32, (tk, tk), 0)
    col = lax.broadcasted_iota(jnp.int32, (tk, tk), 1)
    suffix = jnp.where(row > col, 1.0, 0.0).astype(BF16)

    def gates(qs, j, diagonal):
        rows = qs[0].shape[0]
        start = pl.multiple_of(j * tk, tk)
        zs = [_dot_nt(q, k_ref[pl.ds(start, tk), cols(h)]) for h, q in zip(heads, qs)]
        past = None
        if diagonal:
            past = (lax.broadcasted_iota(jnp.int32, (rows, tk), 1)
                    < lax.broadcasted_iota(jnp.int32, (rows, tk), 0))
        log_betas, log_keeps = [], []
        for z in zs:
            log_beta = jnp.minimum(z, 0.0) - jnp.log(1.0 + jnp.exp(-jnp.abs(z)))
            log_keep = log_beta - z
            if diagonal:
                log_keep = jnp.where(past, log_keep, 0.0)
            log_betas.append(log_beta)
            log_keeps.append(log_keep)
        sums = [_dot(lk.astype(BF16), suffix) for lk in log_keeps]
        return start, past, log_betas, log_keeps, sums

    def accumulate(staged, accs, carries):
        start, past, log_betas, log_keeps, sums = staged
        weights = []
        for log_beta, between, carry in zip(log_betas, sums, carries):
            a = jnp.exp(log_beta + between + carry)
            if past is not None:
                a = jnp.where(past, a, 0.0)
            weights.append(a.astype(BF16))
        accs = [acc + _dot(a, v_ref[pl.ds(start, tk), cols(h)]) for h, a, acc in zip(heads, weights, accs)]
        carries = [c + jnp.sum(lk, axis=1, keepdims=True) for c, lk in zip(carries, log_keeps)]
        return accs, carries

    qs = [q_ref[:, cols(h)] for h in heads]
    right = gates([q_ref[tk:, cols(h)] for h in heads], 2 * i + 1, True)
    left = gates(qs, 2 * i, True)
    accs, carries = accumulate(right, [jnp.zeros((tk, HEAD_DIM), F32) for _ in heads],
                               [jnp.zeros((tk, 1), F32) for _ in heads])
    accs = [jnp.concatenate([jnp.zeros((tk, HEAD_DIM), F32), a], axis=0) for a in accs]
    carries = [jnp.concatenate([jnp.zeros((tk, 1), F32), c], axis=0) for c in carries]
    accs, carries = accumulate(left, accs, carries)

    def alive(carries):
        return jnp.max(functools.reduce(jnp.maximum, carries)) > SB_DEAD_LOG_WEIGHT

    def cond(state):
        j, live, _, _ = state
        return jnp.logical_and(j >= 0, live)

    def body(state):
        j, _, accs, carries = state
        accs, carries = accumulate(gates(qs, j, False), accs, carries)
        return j - 1, alive(carries), accs, carries

    _, _, accs, _ = lax.while_loop(cond, body, (2 * i - 1, alive(carries), accs, carries))
    for h, acc in zip(heads, accs):
        o_ref[:, cols(h)] = acc.astype(o_ref.dtype)


def _chunk_rows(rows, max_chunks):
    return next(r for r in range(BF16_SUBLANES, rows + 1, BF16_SUBLANES)
                if rows % r == 0 and rows // r <= max_chunks)


def _sb_attention(proj3d, stacked_weights):
    b, s, _ = proj3d.shape
    blk = SB_Q_BLOCK
    width = SB_HEADS_PER_STEP * HEAD_DIM
    groups = W_ATT // width
    q_blocks = s // blk
    n_steps = b * groups * q_blocks

    in_cast, out_cast, cast_shapes = [], [], []
    for w in stacked_weights:
        _, rows, cols = w.shape
        chunk = _chunk_rows(rows, n_steps)
        last = rows // chunk - 1

        def chunk_index(bi, g, i, last=last):
            return jnp.minimum((bi * groups + g) * q_blocks + i, last)

        in_cast.append(pl.BlockSpec((None, chunk, cols), lambda bi, g, i, f=chunk_index: (0, f(bi, g, i), 0)))
        out_cast.append(pl.BlockSpec((chunk, cols), lambda bi, g, i, f=chunk_index: (f(bi, g, i), 0)))
        cast_shapes.append(jax.ShapeDtypeStruct((rows, cols), BF16))

    y_spec = pl.BlockSpec((None, blk, width), lambda bi, g, i: (bi, i, g))
    y, *cast = pl.pallas_call(
        _sb_kernel,
        grid=(b, groups, q_blocks),
        in_specs=[
            y_spec,
            pl.BlockSpec((None, s, width), lambda bi, g, i: (bi, 0, groups + g)),
            pl.BlockSpec((None, s, width), lambda bi, g, i: (bi, 0, 2 * groups + g)),
        ] + in_cast,
        out_specs=[y_spec] + out_cast,
        out_shape=[jax.ShapeDtypeStruct((b, s, W_ATT), BF16)] + cast_shapes,
        compiler_params=_params("arbitrary", "arbitrary", "arbitrary"),
        name="sb_attention",
    )(proj3d, proj3d, proj3d, *stacked_weights)
    return y, cast


def _ca_bias_table(rel_bias):
    blk, width = CA_BLOCK, CA_KEY_BLOCKS * CA_BLOCK
    left = LEFT_CHUNKS * CHUNK
    h = rel_bias.shape[0]
    rb = rel_bias.astype(F32)
    n_lo = left + blk - 1 - REL_CLIP
    n_hi = blk - CHUNK + 1
    f = jnp.concatenate([jnp.broadcast_to(rb[:, :1], (h, n_lo)), rb,
                         jnp.broadcast_to(rb[:, -1:], (h, n_hi))], axis=1)
    n = f.shape[1]
    skew = jnp.tile(f, (1, blk))[:, :blk * (n - 1)].reshape(h, blk, n - 1)
    toeplitz = skew[:, :, blk - 1:blk - 1 + width]
    t = jnp.arange(blk)[:, None]
    c = jnp.arange(width)[None, :]
    dchunk = c // CHUNK - LEFT_CHUNKS - t // CHUNK
    valid = (dchunk >= -LEFT_CHUNKS) & (dchunk <= 0)
    return jnp.where(valid[None], toeplitz, NEG)


def _ca_kernel(q_ref, k_ref, v_ref, t_ref, o_ref):
    blk, nb = CA_BLOCK, CA_KEY_BLOCKS

    def rows(i):
        return slice(i * blk, (i + 1) * blk)

    def keys(i):
        return slice(max(0, i - (nb - 1)) * blk, (i + 1) * blk)

    def cols(h):
        return slice(h * HEAD_DIM, (h + 1) * HEAD_DIM)

    def scores(task):
        h, i = task
        nk = i - max(0, i - (nb - 1)) + 1
        return _dot_nt(q_ref[rows(i), cols(h)], k_ref[keys(i), cols(h)]) + t_ref[h, :, (nb - nk) * blk:]

    def softmax_weights(zs):
        ms = [jnp.max(z, axis=1, keepdims=True) for z in zs]
        es = [jnp.exp(z - m) for z, m in zip(zs, ms)]
        return [e.astype(BF16) for e in es], [jnp.sum(e, axis=1, keepdims=True) for e in es]

    def finish(group, es, denoms):
        outs = [_dot(e, v_ref[keys(i), cols(h)]) for e, (h, i) in zip(es, group)]
        for (h, i), out, denom in zip(group, outs, denoms):
            o_ref[rows(i), cols(h)] = (out / denom).astype(o_ref.dtype)

    tasks = [(h, i) for h in range(CA_HEADS_PER_STEP) for i in range(q_ref.shape[0] // blk)]
    groups = [tasks[g:g + CA_GROUP] for g in range(0, len(tasks), CA_GROUP)]
    zs = [scores(task) for task in groups[0]]
    pending = None
    for n, group in enumerate(groups):
        next_zs = [scores(task) for task in groups[n + 1]] if n + 1 < len(groups) else None
        weights = softmax_weights(zs)
        if pending is not None:
            finish(*pending)
        pending = (group, *weights)
        zs = next_zs
    finish(*pending)


def _ca_attention(proj3d, table):
    b, s, _ = proj3d.shape
    width = CA_HEADS_PER_STEP * HEAD_DIM
    groups = W_ATT // width

    def seq_spec(first_group):
        return pl.BlockSpec((None, s, width), lambda bi, g: (bi, 0, first_group + g))

    return pl.pallas_call(
        _ca_kernel,
        grid=(b, groups),
        in_specs=[
            seq_spec(3 * groups),
            seq_spec(4 * groups),
            seq_spec(5 * groups),
            pl.BlockSpec((CA_HEADS_PER_STEP,) + table.shape[1:], lambda bi, g: (g, 0, 0)),
        ],
        out_specs=seq_spec(0),
        out_shape=jax.ShapeDtypeStruct((b, s, W_ATT), BF16),
        compiler_params=_params("parallel", "parallel"),
        name="ca_attention",
    )(proj3d, proj3d, proj3d, table)


def _resident(shape):
    return pl.BlockSpec(shape, lambda *_: (0,) * len(shape), pipeline_mode=pl.Buffered(1))


def _mix_kernel(ysb_ref, yca_ref, gsb_ref, gca_ref, x_ref, wsb_ref, wca_ref, wmix_ref, g_ref,
                x1_ref, h_ref):
    sb = _dot(ysb_ref[...], wsb_ref[...])
    ca = _dot(yca_ref[...], wca_ref[...])
    merged = (jax.nn.sigmoid(gsb_ref[...].astype(F32)) * sb
              + jax.nn.sigmoid(gca_ref[...].astype(F32)) * ca)
    x1 = x_ref[...] + _dot(merged.astype(BF16), wmix_ref[...])
    x1_ref[...] = x1
    h_ref[...] = _rmsnorm_f32(x1, g_ref[...]).astype(BF16)


def _mix_out(y_sb, y_ca, proj2d, x2d, w_sb, w_ca, w_mix, g_ffn):
    t = x2d.shape[0]
    tm = TM_MIX
    gate_blk = 6 * W_ATT // D_MODEL
    att_block = pl.BlockSpec((tm, W_ATT), lambda i: (i, 0))
    row_block = pl.BlockSpec((tm, D_MODEL), lambda i: (i, 0))
    return pl.pallas_call(
        _mix_kernel,
        grid=(t // tm,),
        in_specs=[
            att_block,
            att_block,
            pl.BlockSpec((tm, D_MODEL), lambda i: (i, gate_blk)),
            pl.BlockSpec((tm, D_MODEL), lambda i: (i, gate_blk + 1)),
            row_block,
            _resident((W_ATT, D_MODEL)),
            _resident((W_ATT, D_MODEL)),
            _resident((D_MODEL, D_MODEL)),
            _resident((1, D_MODEL)),
        ],
        out_specs=[row_block, row_block],
        out_shape=[jax.ShapeDtypeStruct((t, D_MODEL), F32), jax.ShapeDtypeStruct((t, D_MODEL), BF16)],
        compiler_params=_params("parallel"),
        name="mix_out",
    )(y_sb, y_ca, proj2d, proj2d, x2d, w_sb, w_ca, w_mix, g_ffn)


def _ffn_kernel(h_ref, wg_ref, wu_ref, wo_ref, o_ref):
    @pl.when(pl.program_id(1) == 0)
    def _():
        o_ref[...] = jnp.zeros_like(o_ref)

    h = h_ref[...]
    gate = _dot(h, wg_ref[...])
    up = _dot(h, wu_ref[...])
    act = (gate * jax.nn.sigmoid(gate) * up).astype(BF16)
    for c in range(0, D_MODEL, TN_FFN_OUT):
        cols = slice(c, c + TN_FFN_OUT)
        o_ref[:, cols] += _dot(act, wo_ref[:, cols])


def _ffn(h, w_in, w_out):
    t = h.shape[0]
    tm, tf = TM_FFN, TF_FFN
    nf = D_FF // tf
    return pl.pallas_call(
        _ffn_kernel,
        grid=(t // tm, nf),
        in_specs=[
            pl.BlockSpec((tm, D_MODEL), lambda i, f: (i, 0)),
            pl.BlockSpec((D_MODEL, tf), lambda i, f: (0, f)),
            pl.BlockSpec((D_MODEL, tf), lambda i, f: (0, nf + f)),
            pl.BlockSpec((tf, D_MODEL), lambda i, f: (f, 0)),
        ],
        out_specs=pl.BlockSpec((tm, D_MODEL), lambda i, f: (i, 0)),
        out_shape=jax.ShapeDtypeStruct((t, D_MODEL), F32),
        compiler_params=_params("parallel", "arbitrary"),
        name="ffn",
    )(h, w_in, w_in, w_out)


def _ple_kernel(x1_ref, y_ref, p_ref, gp_ref, gf_ref, wgate_ref, wple_ref, o_ref):
    x = x1_ref[...] + y_ref[...]
    h = _rmsnorm_f32(x, gp_ref[...]).astype(BF16)
    gate = jax.nn.sigmoid(_dot(h, wgate_ref[...]))
    emb = _dot(p_ref[...].astype(BF16), wple_ref[...])
    o_ref[...] = _rmsnorm_f32(x + gate * emb, gf_ref[...])


def _ple_final(x1, y_ffn, p_stacked, g_ple, g_final, w_gate, w_ple):
    t = x1.shape[0]
    tm = TM_PLE
    blocks_per_seq = p_stacked.shape[2] // tm
    row_block = pl.BlockSpec((tm, D_MODEL), lambda i: (i, 0))
    return pl.pallas_call(
        _ple_kernel,
        grid=(t // tm,),
        in_specs=[
            row_block,
            row_block,
            pl.BlockSpec((None, None, tm, D_PLE),
                         lambda i: (0, i // blocks_per_seq, i % blocks_per_seq, 0)),
            _resident((1, D_MODEL)),
            _resident((1, D_MODEL)),
            _resident((D_MODEL, D_MODEL)),
            _resident((D_PLE, D_MODEL)),
        ],
        out_specs=row_block,
        out_shape=jax.ShapeDtypeStruct((t, D_MODEL), F32),
        compiler_params=_params("parallel"),
        name="ple_final",
    )(x1, y_ffn, p_stacked, g_ple, g_final, w_gate, w_ple)


def kernel(x, p, w_in, w_sb_out, w_ca_out, w_mix_out, rel_bias, g_mix, g_ffn, g_ple, g_final,
           w_ffn_in, w_ffn_out, w_ple_in, w_ple_gate):
    b, s, d = x.shape
    assert w_in.shape[0] == 1, "the output norm is fused into the (single) layer's last kernel"
    xt = x.reshape(b * s, d)
    proj = _in_proj(xt, g_mix[0][None], w_in[0].astype(BF16))
    proj3d = proj.reshape(b, s, IN_COLS)
    y_sb, (w_sb, w_ca, w_mix, w_ffn_a, w_ffn_b, w_gate, w_ple) = _sb_attention(
        proj3d, [w_sb_out, w_ca_out, w_mix_out, w_ffn_in, w_ffn_out, w_ple_gate, w_ple_in])
    y_sb = y_sb.reshape(b * s, W_ATT)
    y_ca = _ca_attention(proj3d, _ca_bias_table(rel_bias[0])).reshape(b * s, W_ATT)
    x1, h_ffn = _mix_out(y_sb, y_ca, proj, xt, w_sb, w_ca, w_mix, g_ffn[0][None])
    y_ffn = _ffn(h_ffn, w_ffn_a, w_ffn_b)
    out = _ple_final(x1, y_ffn, p, g_ple[0][None], g_final[None],
                     w_gate, w_ple)
    return out.reshape(b, s, d)
```

```python
import functools

import jax
import jax.numpy as jnp
from jax import lax
from jax.experimental import pallas as pl
from jax.experimental.pallas import tpu as pltpu

D_MODEL = 2048
CHUNK = 64
HEAD_DIM = 128
N_HEADS = 8
W_ATT = N_HEADS * HEAD_DIM
LEFT_CHUNKS = 8
REL_CLIP = 128
N_REL = REL_CLIP + CHUNK
D_FF = 5632
D_PLE = 256
EPS = 1e-6
NEG = -1e30
IN_COLS = 6 * W_ATT + 2 * D_MODEL

BF16 = jnp.bfloat16
F32 = jnp.float32

BF16_SUBLANES = 16

VMEM_LIMIT_BYTES = 52 * 1024 * 1024

TM_INPROJ, TN_INPROJ = 1024, 2048
TM_MIX = 256
TM_FFN, TF_FFN, TN_FFN_OUT = 1024, 512, 512
TM_PLE = 512
SB_Q_BLOCK, SB_K_BLOCK = 512, 256
SB_DEAD_LOG_WEIGHT = -110.0
SB_HEADS_PER_STEP = 4
CA_BLOCK = 128
CA_KEY_BLOCKS = LEFT_CHUNKS * CHUNK // CA_BLOCK + 1
CA_GROUP = 2
CA_HEADS_PER_STEP = 2


def _params(*semantics):
    return pltpu.CompilerParams(dimension_semantics=semantics, vmem_limit_bytes=VMEM_LIMIT_BYTES)


def _rmsnorm_f32(x, g):
    ms = jnp.mean(x * x, axis=-1, keepdims=True)
    return x * lax.rsqrt(ms + EPS) * g


def _dot(a, b):
    return jnp.dot(a, b, preferred_element_type=F32)


def _dot_nt(a, b):
    return lax.dot_general(a, b, (((1,), (1,)), ((), ())), preferred_element_type=F32)


def _query_column_scale():
    col = jnp.arange(IN_COLS)[None, :]
    is_query = (col < W_ATT) | ((col >= 3 * W_ATT) & (col < 4 * W_ATT))
    return jnp.where(is_query, HEAD_DIM ** -0.5, 1.0).astype(F32)


def _inproj_kernel(x_ref, g_ref, w_ref, cs_ref, o_ref, h_ref):
    @pl.when(pl.program_id(1) == 0)
    def _():
        h_ref[...] = _rmsnorm_f32(x_ref[...], g_ref[...]).astype(BF16)

    o_ref[...] = (_dot(h_ref[...], w_ref[...]) * cs_ref[...]).astype(o_ref.dtype)


def _in_proj(x2d, g, w_bf16):
    t, d = x2d.shape
    n = w_bf16.shape[1]
    tm, tn = TM_INPROJ, TN_INPROJ
    return pl.pallas_call(
        _inproj_kernel,
        grid=(t // tm, n // tn),
        in_specs=[
            pl.BlockSpec((tm, d), lambda i, j: (i, 0)),
            pl.BlockSpec((1, d), lambda i, j: (0, 0)),
            pl.BlockSpec((d, tn), lambda i, j: (0, j)),
            pl.BlockSpec((1, tn), lambda i, j: (0, j)),
        ],
        out_specs=pl.BlockSpec((tm, tn), lambda i, j: (i, j)),
        out_shape=jax.ShapeDtypeStruct((t, n), BF16),
        scratch_shapes=[pltpu.VMEM((tm, d), BF16)],
        compiler_params=_params("parallel", "arbitrary"),
        name="in_proj",
    )(x2d, g, w_bf16, _query_column_scale())


def _sb_kernel(q_ref, k_ref, v_ref, *refs):
    n_cast = len(refs) // 2
    o_ref = refs[n_cast]
    for src_ref, dst_ref in zip(refs[:n_cast], refs[n_cast + 1:]):
        if len(dst_ref.shape) == 2:
            dst_ref[...] = src_ref[...].astype(dst_ref.dtype)
        else:
            width = dst_ref.shape[2]
            for t in range(dst_ref.shape[0]):
                dst_ref[t] = src_ref[:, t * width:(t + 1) * width].astype(dst_ref.dtype)

    tq, tk = SB_Q_BLOCK, SB_K_BLOCK
    assert tq == 2 * tk
    heads = range(SB_HEADS_PER_STEP)
    i = pl.program_id(2)

    def cols(h):
        return slice(h * HEAD_DIM, (h + 1) * HEAD_DIM)

    row = lax.broadcasted_iota(jnp.int32, (tk, tk), 0)
    col = lax.broadcasted_iota(jnp.int32, (tk, tk), 1)
    suffix = jnp.where(row > col, 1.0, 0.0).astype(BF16)

    def gates(qs, j, diagonal):
        rows = qs[0].shape[0]
        start = pl.multiple_of(j * tk, tk)
        zs = [_dot_nt(q, k_ref[pl.ds(start, tk), cols(h)]) for h, q in zip(heads, qs)]
        past = None
        if diagonal:
            past = (lax.broadcasted_iota(jnp.int32, (rows, tk), 1)
                    < lax.broadcasted_iota(jnp.int32, (rows, tk), 0))
        log_betas, log_keeps = [], []
        for z in zs:
            log_beta = jnp.minimum(z, 0.0) - jnp.log(1.0 + jnp.exp(-jnp.abs(z)))
            log_keep = log_beta - z
            if diagonal:
                log_keep = jnp.where(past, log_keep, 0.0)
            log_betas.append(log_beta)
            log_keeps.append(log_keep)
        sums = [_dot(lk.astype(BF16), suffix) for lk in log_keeps]
        return start, past, log_betas, log_keeps, sums

    def accumulate(staged, accs, carries):
        start, past, log_betas, log_keeps, sums = staged
        weights = []
        for log_beta, between, carry in zip(log_betas, sums, carries):
            a = jnp.exp(log_beta + between + carry)
            if past is not None:
                a = jnp.where(past, a, 0.0)
            weights.append(a.astype(BF16))
        accs = [acc + _dot(a, v_ref[pl.ds(start, tk), cols(h)]) for h, a, acc in zip(heads, weights, accs)]
        carries = [c + jnp.sum(lk, axis=1, keepdims=True) for c, lk in zip(carries, log_keeps)]
        return accs, carries

    qs = [q_ref[:, cols(h)] for h in heads]
    right = gates([q_ref[tk:, cols(h)] for h in heads], 2 * i + 1, True)
    left = gates(qs, 2 * i, True)
    accs, carries = accumulate(right, [jnp.zeros((tk, HEAD_DIM), F32) for _ in heads],
                               [jnp.zeros((tk, 1), F32) for _ in heads])
    accs = [jnp.concatenate([jnp.zeros((tk, HEAD_DIM), F32), a], axis=0) for a in accs]
    carries = [jnp.concatenate([jnp.zeros((tk, 1), F32), c], axis=0) for c in carries]
    accs, carries = accumulate(left, accs, carries)

    def alive(carries):
        return jnp.max(functools.reduce(jnp.maximum, carries)) > SB_DEAD_LOG_WEIGHT

    def cond(state):
        j, live, _, _ = state
        return jnp.logical_and(j >= 0, live)

    def body(state):
        j, _, accs, carries = state
        accs, carries = accumulate(gates(qs, j, False), accs, carries)
        return j - 1, alive(carries), accs, carries

    _, _, accs, _ = lax.while_loop(cond, body, (2 * i - 1, alive(carries), accs, carries))
    for h, acc in zip(heads, accs):
        o_ref[:, cols(h)] = acc.astype(o_ref.dtype)


def _chunk_rows(rows, max_chunks):
    return next(r for r in range(BF16_SUBLANES, rows + 1, BF16_SUBLANES)
                if rows % r == 0 and rows // r <= max_chunks)


def _sb_attention(proj3d, stacked_weights, column_tiles):
    b, s, _ = proj3d.shape
    blk = SB_Q_BLOCK
    width = SB_HEADS_PER_STEP * HEAD_DIM
    groups = W_ATT // width
    q_blocks = s // blk
    n_steps = b * groups * q_blocks

    in_cast, out_cast, cast_shapes = [], [], []
    for w, tile in zip(stacked_weights, column_tiles):
        _, rows, cols = w.shape
        chunk = _chunk_rows(rows, n_steps)
        last = rows // chunk - 1

        def chunk_index(bi, g, i, last=last):
            return jnp.minimum((bi * groups + g) * q_blocks + i, last)

        in_cast.append(pl.BlockSpec((None, chunk, cols), lambda bi, g, i, f=chunk_index: (0, f(bi, g, i), 0)))
        if tile is None:
            out_cast.append(pl.BlockSpec((chunk, cols), lambda bi, g, i, f=chunk_index: (f(bi, g, i), 0)))
            cast_shapes.append(jax.ShapeDtypeStruct((rows, cols), BF16))
        else:
            out_cast.append(pl.BlockSpec((cols // tile, chunk, tile),
                                         lambda bi, g, i, f=chunk_index: (0, f(bi, g, i), 0)))
            cast_shapes.append(jax.ShapeDtypeStruct((cols // tile, rows, tile), BF16))

    y_spec = pl.BlockSpec((None, blk, width), lambda bi, g, i: (bi, i, g))
    y, *cast = pl.pallas_call(
        _sb_kernel,
        grid=(b, groups, q_blocks),
        in_specs=[
            y_spec,
            pl.BlockSpec((None, s, width), lambda bi, g, i: (bi, 0, groups + g)),
            pl.BlockSpec((None, s, width), lambda bi, g, i: (bi, 0, 2 * groups + g)),
        ] + in_cast,
        out_specs=[y_spec] + out_cast,
        out_shape=[jax.ShapeDtypeStruct((b, s, W_ATT), BF16)] + cast_shapes,
        compiler_params=_params("arbitrary", "arbitrary", "arbitrary"),
        name="sb_attention",
    )(proj3d, proj3d, proj3d, *stacked_weights)
    return y, cast


def _ca_bias_table(rel_bias):
    blk, width = CA_BLOCK, CA_KEY_BLOCKS * CA_BLOCK
    left = LEFT_CHUNKS * CHUNK
    h = rel_bias.shape[0]
    rb = rel_bias.astype(F32)
    n_lo = left + blk - 1 - REL_CLIP
    n_hi = blk - CHUNK + 1
    f = jnp.concatenate([jnp.broadcast_to(rb[:, :1], (h, n_lo)), rb,
                         jnp.broadcast_to(rb[:, -1:], (h, n_hi))], axis=1)
    n = f.shape[1]
    skew = jnp.tile(f, (1, blk))[:, :blk * (n - 1)].reshape(h, blk, n - 1)
    toeplitz = skew[:, :, blk - 1:blk - 1 + width]
    t = jnp.arange(blk)[:, None]
    c = jnp.arange(width)[None, :]
    dchunk = c // CHUNK - LEFT_CHUNKS - t // CHUNK
    valid = (dchunk >= -LEFT_CHUNKS) & (dchunk <= 0)
    return jnp.where(valid[None], toeplitz, NEG)


def _ca_kernel(q_ref, k_ref, v_ref, t_ref, o_ref):
    blk, nb = CA_BLOCK, CA_KEY_BLOCKS

    def rows(i):
        return slice(i * blk, (i + 1) * blk)

    def keys(i):
        return slice(max(0, i - (nb - 1)) * blk, (i + 1) * blk)

    def cols(h):
        return slice(h * HEAD_DIM, (h + 1) * HEAD_DIM)

    def scores(task):
        h, i = task
        nk = i - max(0, i - (nb - 1)) + 1
        return _dot_nt(q_ref[rows(i), cols(h)], k_ref[keys(i), cols(h)]) + t_ref[h, :, (nb - nk) * blk:]

    def softmax_weights(zs):
        ms = [jnp.max(z, axis=1, keepdims=True) for z in zs]
        es = [jnp.exp(z - m) for z, m in zip(zs, ms)]
        return [e.astype(BF16) for e in es], [jnp.sum(e, axis=1, keepdims=True) for e in es]

    def finish(group, es, denoms):
        outs = [_dot(e, v_ref[keys(i), cols(h)]) for e, (h, i) in zip(es, group)]
        for (h, i), out, denom in zip(group, outs, denoms):
            o_ref[rows(i), cols(h)] = (out / denom).astype(o_ref.dtype)

    tasks = [(h, i) for h in range(CA_HEADS_PER_STEP) for i in range(q_ref.shape[0] // blk)]
    groups = [tasks[g:g + CA_GROUP] for g in range(0, len(tasks), CA_GROUP)]
    zs = [scores(task) for task in groups[0]]
    pending = None
    for n, group in enumerate(groups):
        next_zs = [scores(task) for task in groups[n + 1]] if n + 1 < len(groups) else None
        weights = softmax_weights(zs)
        if pending is not None:
            finish(*pending)
        pending = (group, *weights)
        zs = next_zs
    finish(*pending)


def _ca_attention(proj3d, table):
    b, s, _ = proj3d.shape
    width = CA_HEADS_PER_STEP * HEAD_DIM
    groups = W_ATT // width

    def seq_spec(first_group):
        return pl.BlockSpec((None, s, width), lambda bi, g: (bi, 0, first_group + g))

    return pl.pallas_call(
        _ca_kernel,
        grid=(b, groups),
        in_specs=[
            seq_spec(3 * groups),
            seq_spec(4 * groups),
            seq_spec(5 * groups),
            pl.BlockSpec((CA_HEADS_PER_STEP,) + table.shape[1:], lambda bi, g: (g, 0, 0)),
        ],
        out_specs=seq_spec(0),
        out_shape=jax.ShapeDtypeStruct((b, s, W_ATT), BF16),
        compiler_params=_params("parallel", "parallel"),
        name="ca_attention",
    )(proj3d, proj3d, proj3d, table)


def _resident(shape):
    return pl.BlockSpec(shape, lambda *_: (0,) * len(shape), pipeline_mode=pl.Buffered(1))


def _mix_kernel(ysb_ref, yca_ref, gsb_ref, gca_ref, x_ref, wsb_ref, wca_ref, wmix_ref, g_ref,
                x1_ref, h_ref):
    sb = _dot(ysb_ref[...], wsb_ref[...])
    ca = _dot(yca_ref[...], wca_ref[...])
    merged = (jax.nn.sigmoid(gsb_ref[...].astype(F32)) * sb
              + jax.nn.sigmoid(gca_ref[...].astype(F32)) * ca)
    x1 = x_ref[...] + _dot(merged.astype(BF16), wmix_ref[...])
    x1_ref[...] = x1
    h_ref[...] = _rmsnorm_f32(x1, g_ref[...]).astype(BF16)


def _mix_out(y_sb, y_ca, proj2d, x2d, w_sb, w_ca, w_mix, g_ffn):
    t = x2d.shape[0]
    tm = TM_MIX
    gate_blk = 6 * W_ATT // D_MODEL
    att_block = pl.BlockSpec((tm, W_ATT), lambda i: (i, 0))
    row_block = pl.BlockSpec((tm, D_MODEL), lambda i: (i, 0))
    return pl.pallas_call(
        _mix_kernel,
        grid=(t // tm,),
        in_specs=[
            att_block,
            att_block,
            pl.BlockSpec((tm, D_MODEL), lambda i: (i, gate_blk)),
            pl.BlockSpec((tm, D_MODEL), lambda i: (i, gate_blk + 1)),
            row_block,
            _resident((W_ATT, D_MODEL)),
            _resident((W_ATT, D_MODEL)),
            _resident((D_MODEL, D_MODEL)),
            _resident((1, D_MODEL)),
        ],
        out_specs=[row_block, row_block],
        out_shape=[jax.ShapeDtypeStruct((t, D_MODEL), F32), jax.ShapeDtypeStruct((t, D_MODEL), BF16)],
        compiler_params=_params("parallel"),
        name="mix_out",
    )(y_sb, y_ca, proj2d, proj2d, x2d, w_sb, w_ca, w_mix, g_ffn)


def _ffn_kernel(h_ref, wg_ref, wu_ref, wo_ref, o_ref):
    @pl.when(pl.program_id(1) == 0)
    def _():
        o_ref[...] = jnp.zeros_like(o_ref)

    h = h_ref[...]
    gate = _dot(h, wg_ref[...])
    up = _dot(h, wu_ref[...])
    act = (gate * jax.nn.sigmoid(gate) * up).astype(BF16)
    for c in range(0, D_MODEL, TN_FFN_OUT):
        cols = slice(c, c + TN_FFN_OUT)
        o_ref[:, cols] += _dot(act, wo_ref[:, cols])


def _ffn(h, w_in, w_out):
    t = h.shape[0]
    tm, tf = TM_FFN, TF_FFN
    nf = D_FF // tf
    return pl.pallas_call(
        _ffn_kernel,
        grid=(t // tm, nf),
        in_specs=[
            pl.BlockSpec((tm, D_MODEL), lambda i, f: (i, 0)),
            pl.BlockSpec((None, D_MODEL, tf), lambda i, f: (f, 0, 0)),
            pl.BlockSpec((None, D_MODEL, tf), lambda i, f: (nf + f, 0, 0)),
            pl.BlockSpec((tf, D_MODEL), lambda i, f: (f, 0)),
        ],
        out_specs=pl.BlockSpec((tm, D_MODEL), lambda i, f: (i, 0)),
        out_shape=jax.ShapeDtypeStruct((t, D_MODEL), F32),
        compiler_params=_params("parallel", "arbitrary"),
        name="ffn",
    )(h, w_in, w_in, w_out)


def _ple_kernel(x1_ref, y_ref, p_ref, gp_ref, gf_ref, wgate_ref, wple_ref, o_ref):
    x = x1_ref[...] + y_ref[...]
    h = _rmsnorm_f32(x, gp_ref[...]).astype(BF16)
    gate = jax.nn.sigmoid(_dot(h, wgate_ref[...]))
    emb = _dot(p_ref[...].astype(BF16), wple_ref[...])
    o_ref[...] = _rmsnorm_f32(x + gate * emb, gf_ref[...])


def _ple_final(x1, y_ffn, p_stacked, g_ple, g_final, w_gate, w_ple):
    t = x1.shape[0]
    tm = TM_PLE
    blocks_per_seq = p_stacked.shape[2] // tm
    row_block = pl.BlockSpec((tm, D_MODEL), lambda i: (i, 0))
    return pl.pallas_call(
        _ple_kernel,
        grid=(t // tm,),
        in_specs=[
            row_block,
            row_block,
            pl.BlockSpec((None, None, tm, D_PLE),
                         lambda i: (0, i // blocks_per_seq, i % blocks_per_seq, 0)),
            _resident((1, D_MODEL)),
            _resident((1, D_MODEL)),
            _resident((D_MODEL, D_MODEL)),
            _resident((D_PLE, D_MODEL)),
        ],
        out_specs=row_block,
        out_shape=jax.ShapeDtypeStruct((t, D_MODEL), F32),
        compiler_params=_params("parallel"),
        name="ple_final",
    )(x1, y_ffn, p_stacked, g_ple, g_final, w_gate, w_ple)


def kernel(x, p, w_in, w_sb_out, w_ca_out, w_mix_out, rel_bias, g_mix, g_ffn, g_ple, g_final,
           w_ffn_in, w_ffn_out, w_ple_in, w_ple_gate):
    b, s, d = x.shape
    assert w_in.shape[0] == 1, "the output norm is fused into the (single) layer's last kernel"
    xt = x.reshape(b * s, d)
    proj = _in_proj(xt, g_mix[0][None], w_in[0].astype(BF16))
    proj3d = proj.reshape(b, s, IN_COLS)
    y_sb, (w_sb, w_ca, w_mix, w_ffn_a, w_ffn_b, w_gate, w_ple) = _sb_attention(
        proj3d, [w_sb_out, w_ca_out, w_mix_out, w_ffn_in, w_ffn_out, w_ple_gate, w_ple_in],
        [None, None, None, TF_FFN, None, None, None])
    y_sb = y_sb.reshape(b * s, W_ATT)
    y_ca = _ca_attention(proj3d, _ca_bias_table(rel_bias[0])).reshape(b * s, W_ATT)
    x1, h_ffn = _mix_out(y_sb, y_ca, proj, xt, w_sb, w_ca, w_mix, g_ffn[0][None])
    y_ffn = _ffn(h_ffn, w_ffn_a, w_ffn_b)
    out = _ple_final(x1, y_ffn, p, g_ple[0][None], g_final[None],
                     w_gate, w_ple)
    return out.reshape(b, s, d)
```

```python
import functools

import jax
import jax.numpy as jnp
from jax import lax
from jax.experimental import pallas as pl
from jax.experimental.pallas import tpu as pltpu

D_MODEL = 2048
CHUNK = 64
HEAD_DIM = 128
N_HEADS = 8
W_ATT = N_HEADS * HEAD_DIM
LEFT_CHUNKS = 8
REL_CLIP = 128
N_REL = REL_CLIP + CHUNK
D_FF = 5632
D_PLE = 256
EPS = 1e-6
NEG = -1e30
IN_COLS = 6 * W_ATT + 2 * D_MODEL

BF16 = jnp.bfloat16
F32 = jnp.float32

BF16_SUBLANES = 16

VMEM_LIMIT_BYTES = 52 * 1024 * 1024

TM_INPROJ, TN_INPROJ = 1024, 2048
TM_MIX = 256
TM_FFN, TF_FFN, TN_FFN_OUT = 1024, 512, 512
TM_PLE = 512
SB_Q_BLOCK, SB_K_BLOCK = 512, 256
SB_DEAD_LOG_WEIGHT = -110.0
SB_HEADS_PER_STEP = 4
CA_BLOCK = 128
CA_KEY_BLOCKS = LEFT_CHUNKS * CHUNK // CA_BLOCK + 1
CA_GROUP = 2
CA_HEADS_PER_STEP = 4


def _params(*semantics):
    return pltpu.CompilerParams(dimension_semantics=semantics, vmem_limit_bytes=VMEM_LIMIT_BYTES)


def _rmsnorm_f32(x, g):
    ms = jnp.mean(x * x, axis=-1, keepdims=True)
    return x * lax.rsqrt(ms + EPS) * g


def _dot(a, b):
    return jnp.dot(a, b, preferred_element_type=F32)


def _dot_nt(a, b):
    return lax.dot_general(a, b, (((1,), (1,)), ((), ())), preferred_element_type=F32)


def _query_column_scale():
    col = jnp.arange(IN_COLS)[None, :]
    is_query = (col < W_ATT) | ((col >= 3 * W_ATT) & (col < 4 * W_ATT))
    return jnp.where(is_query, HEAD_DIM ** -0.5, 1.0).astype(F32)


def _inproj_kernel(x_ref, g_ref, w_ref, cs_ref, o_ref, h_ref):
    @pl.when(pl.program_id(1) == 0)
    def _():
        h_ref[...] = _rmsnorm_f32(x_ref[...], g_ref[...]).astype(BF16)

    o_ref[...] = (_dot(h_ref[...], w_ref[...]) * cs_ref[...]).astype(o_ref.dtype)


def _in_proj(x2d, g, w_bf16):
    t, d = x2d.shape
    n = w_bf16.shape[1]
    tm, tn = TM_INPROJ, TN_INPROJ
    return pl.pallas_call(
        _inproj_kernel,
        grid=(t // tm, n // tn),
        in_specs=[
            pl.BlockSpec((tm, d), lambda i, j: (i, 0)),
            pl.BlockSpec((1, d), lambda i, j: (0, 0)),
            pl.BlockSpec((d, tn), lambda i, j: (0, j)),
            pl.BlockSpec((1, tn), lambda i, j: (0, j)),
        ],
        out_specs=pl.BlockSpec((tm, tn), lambda i, j: (i, j)),
        out_shape=jax.ShapeDtypeStruct((t, n), BF16),
        scratch_shapes=[pltpu.VMEM((tm, d), BF16)],
        compiler_params=_params("parallel", "arbitrary"),
        name="in_proj",
    )(x2d, g, w_bf16, _query_column_scale())


def _sb_kernel(q_ref, k_ref, v_ref, *refs):
    n_cast = len(refs) // 2
    o_ref = refs[n_cast]
    for src_ref, dst_ref in zip(refs[:n_cast], refs[n_cast + 1:]):
        dst_ref[...] = src_ref[...].astype(dst_ref.dtype)

    tq, tk = SB_Q_BLOCK, SB_K_BLOCK
    assert tq == 2 * tk
    heads = range(SB_HEADS_PER_STEP)
    i = pl.program_id(2)

    def cols(h):
        return slice(h * HEAD_DIM, (h + 1) * HEAD_DIM)

    row = lax.broadcasted_iota(jnp.int32, (tk, tk), 0)
    col = lax.broadcasted_iota(jnp.int32, (tk, tk), 1)
    suffix = jnp.where(row > col, 1.0, 0.0).astype(BF16)

    def gates(qs, j, diagonal):
        rows = qs[0].shape[0]
        start = pl.multiple_of(j * tk, tk)
        zs = [_dot_nt(q, k_ref[pl.ds(start, tk), cols(h)]) for h, q in zip(heads, qs)]
        past = None
        if diagonal:
            past = (lax.broadcasted_iota(jnp.int32, (rows, tk), 1)
                    < lax.broadcasted_iota(jnp.int32, (rows, tk), 0))
        log_betas, log_keeps = [], []
        for z in zs:
            log_beta = jnp.minimum(z, 0.0) - jnp.log(1.0 + jnp.exp(-jnp.abs(z)))
            log_keep = log_beta - z
            if diagonal:
                log_keep = jnp.where(past, log_keep, 0.0)
            log_betas.append(log_beta)
            log_keeps.append(log_keep)
        sums = [_dot(lk.astype(BF16), suffix) for lk in log_keeps]
        return start, past, log_betas, log_keeps, sums

    def accumulate(staged, accs, carries):
        start, past, log_betas, log_keeps, sums = staged
        weights = []
        for log_beta, between, carry in zip(log_betas, sums, carries):
            a = jnp.exp(log_beta + between + carry)
            if past is not None:
                a = jnp.where(past, a, 0.0)
            weights.append(a.astype(BF16))
        accs = [acc + _dot(a, v_ref[pl.ds(start, tk), cols(h)]) for h, a, acc in zip(heads, weights, accs)]
        carries = [c + jnp.sum(lk, axis=1, keepdims=True) for c, lk in zip(carries, log_keeps)]
        return accs, carries

    qs = [q_ref[:, cols(h)] for h in heads]
    right = gates([q_ref[tk:, cols(h)] for h in heads], 2 * i + 1, True)
    left = gates(qs, 2 * i, True)
    accs, carries = accumulate(right, [jnp.zeros((tk, HEAD_DIM), F32) for _ in heads],
                               [jnp.zeros((tk, 1), F32) for _ in heads])
    accs = [jnp.concatenate([jnp.zeros((tk, HEAD_DIM), F32), a], axis=0) for a in accs]
    carries = [jnp.concatenate([jnp.zeros((tk, 1), F32), c], axis=0) for c in carries]
    accs, carries = accumulate(left, accs, carries)

    def alive(carries):
        return jnp.max(functools.reduce(jnp.maximum, carries)) > SB_DEAD_LOG_WEIGHT

    def cond(state):
        j, live, _, _ = state
        return jnp.logical_and(j >= 0, live)

    def body(state):
        j, _, accs, carries = state
        accs, carries = accumulate(gates(qs, j, False), accs, carries)
        return j - 1, alive(carries), accs, carries

    _, _, accs, _ = lax.while_loop(cond, body, (2 * i - 1, alive(carries), accs, carries))
    for h, acc in zip(heads, accs):
        o_ref[:, cols(h)] = acc.astype(o_ref.dtype)


def _chunk_rows(rows, max_chunks):
    return next(r for r in range(BF16_SUBLANES, rows + 1, BF16_SUBLANES)
                if rows % r == 0 and rows // r <= max_chunks)


def _sb_attention(proj3d, stacked_weights):
    b, s, _ = proj3d.shape
    blk = SB_Q_BLOCK
    width = SB_HEADS_PER_STEP * HEAD_DIM
    groups = W_ATT // width
    q_blocks = s // blk
    n_steps = b * groups * q_blocks

    in_cast, out_cast, cast_shapes = [], [], []
    for w in stacked_weights:
        _, rows, cols = w.shape
        chunk = _chunk_rows(rows, n_steps)
        last = rows // chunk - 1

        def chunk_index(bi, g, i, last=last):
            return jnp.minimum((bi * groups + g) * q_blocks + i, last)

        in_cast.append(pl.BlockSpec((None, chunk, cols), lambda bi, g, i, f=chunk_index: (0, f(bi, g, i), 0)))
        out_cast.append(pl.BlockSpec((chunk, cols), lambda bi, g, i, f=chunk_index: (f(bi, g, i), 0)))
        cast_shapes.append(jax.ShapeDtypeStruct((rows, cols), BF16))

    y_spec = pl.BlockSpec((None, blk, width), lambda bi, g, i: (bi, i, g))
    y, *cast = pl.pallas_call(
        _sb_kernel,
        grid=(b, groups, q_blocks),
        in_specs=[
            y_spec,
            pl.BlockSpec((None, s, width), lambda bi, g, i: (bi, 0, groups + g)),
            pl.BlockSpec((None, s, width), lambda bi, g, i: (bi, 0, 2 * groups + g)),
        ] + in_cast,
        out_specs=[y_spec] + out_cast,
        out_shape=[jax.ShapeDtypeStruct((b, s, W_ATT), BF16)] + cast_shapes,
        compiler_params=_params("arbitrary", "arbitrary", "arbitrary"),
        name="sb_attention",
    )(proj3d, proj3d, proj3d, *stacked_weights)
    return y, cast


def _ca_bias_table(rel_bias):
    blk, width = CA_BLOCK, CA_KEY_BLOCKS * CA_BLOCK
    left = LEFT_CHUNKS * CHUNK
    h = rel_bias.shape[0]
    rb = rel_bias.astype(F32)
    n_lo = left + blk - 1 - REL_CLIP
    n_hi = blk - CHUNK + 1
    f = jnp.concatenate([jnp.broadcast_to(rb[:, :1], (h, n_lo)), rb,
                         jnp.broadcast_to(rb[:, -1:], (h, n_hi))], axis=1)
    n = f.shape[1]
    skew = jnp.tile(f, (1, blk))[:, :blk * (n - 1)].reshape(h, blk, n - 1)
    toeplitz = skew[:, :, blk - 1:blk - 1 + width]
    t = jnp.arange(blk)[:, None]
    c = jnp.arange(width)[None, :]
    dchunk = c // CHUNK - LEFT_CHUNKS - t // CHUNK
    valid = (dchunk >= -LEFT_CHUNKS) & (dchunk <= 0)
    return jnp.where(valid[None], toeplitz, NEG)


def _ca_kernel(q_ref, k_ref, v_ref, t_ref, o_ref):
    blk, nb = CA_BLOCK, CA_KEY_BLOCKS

    def rows(i):
        return slice(i * blk, (i + 1) * blk)

    def keys(i):
        return slice(max(0, i - (nb - 1)) * blk, (i + 1) * blk)

    def cols(h):
        return slice(h * HEAD_DIM, (h + 1) * HEAD_DIM)

    def scores(task):
        h, i = task
        nk = i - max(0, i - (nb - 1)) + 1
        return _dot_nt(q_ref[rows(i), cols(h)], k_ref[keys(i), cols(h)]) + t_ref[h, :, (nb - nk) * blk:]

    def softmax_weights(zs):
        ms = [jnp.max(z, axis=1, keepdims=True) for z in zs]
        es = [jnp.exp(z - m) for z, m in zip(zs, ms)]
        return [e.astype(BF16) for e in es], [jnp.sum(e, axis=1, keepdims=True) for e in es]

    def finish(group, es, denoms):
        outs = [_dot(e, v_ref[keys(i), cols(h)]) for e, (h, i) in zip(es, group)]
        for (h, i), out, denom in zip(group, outs, denoms):
            o_ref[rows(i), cols(h)] = (out / denom).astype(o_ref.dtype)

    tasks = [(h, i) for h in range(CA_HEADS_PER_STEP) for i in range(q_ref.shape[0] // blk)]
    groups = [tasks[g:g + CA_GROUP] for g in range(0, len(tasks), CA_GROUP)]
    zs = [scores(task) for task in groups[0]]
    pending = None
    for n, group in enumerate(groups):
        next_zs = [scores(task) for task in groups[n + 1]] if n + 1 < len(groups) else None
        weights = softmax_weights(zs)
        if pending is not None:
            finish(*pending)
        pending = (group, *weights)
        zs = next_zs
    finish(*pending)


def _ca_attention(proj3d, table):
    b, s, _ = proj3d.shape
    width = CA_HEADS_PER_STEP * HEAD_DIM
    groups = W_ATT // width

    def seq_spec(first_group):
        return pl.BlockSpec((None, s, width), lambda bi, g: (bi, 0, first_group + g))

    return pl.pallas_call(
        _ca_kernel,
        grid=(b, groups),
        in_specs=[
            seq_spec(3 * groups),
            seq_spec(4 * groups),
            seq_spec(5 * groups),
            pl.BlockSpec((CA_HEADS_PER_STEP,) + table.shape[1:], lambda bi, g: (g, 0, 0)),
        ],
        out_specs=seq_spec(0),
        out_shape=jax.ShapeDtypeStruct((b, s, W_ATT), BF16),
        compiler_params=_params("parallel", "parallel"),
        name="ca_attention",
    )(proj3d, proj3d, proj3d, table)


def _resident(shape):
    return pl.BlockSpec(shape, lambda *_: (0,) * len(shape), pipeline_mode=pl.Buffered(1))


def _mix_kernel(ysb_ref, yca_ref, gsb_ref, gca_ref, x_ref, wsb_ref, wca_ref, wmix_ref, g_ref,
                x1_ref, h_ref):
    sb = _dot(ysb_ref[...], wsb_ref[...])
    ca = _dot(yca_ref[...], wca_ref[...])
    merged = (jax.nn.sigmoid(gsb_ref[...].astype(F32)) * sb
              + jax.nn.sigmoid(gca_ref[...].astype(F32)) * ca)
    x1 = x_ref[...] + _dot(merged.astype(BF16), wmix_ref[...])
    x1_ref[...] = x1
    h_ref[...] = _rmsnorm_f32(x1, g_ref[...]).astype(BF16)


def _mix_out(y_sb, y_ca, proj2d, x2d, w_sb, w_ca, w_mix, g_ffn):
    t = x2d.shape[0]
    tm = TM_MIX
    gate_blk = 6 * W_ATT // D_MODEL
    att_block = pl.BlockSpec((tm, W_ATT), lambda i: (i, 0))
    row_block = pl.BlockSpec((tm, D_MODEL), lambda i: (i, 0))
    return pl.pallas_call(
        _mix_kernel,
        grid=(t // tm,),
        in_specs=[
            att_block,
            att_block,
            pl.BlockSpec((tm, D_MODEL), lambda i: (i, gate_blk)),
            pl.BlockSpec((tm, D_MODEL), lambda i: (i, gate_blk + 1)),
            row_block,
            _resident((W_ATT, D_MODEL)),
            _resident((W_ATT, D_MODEL)),
            _resident((D_MODEL, D_MODEL)),
            _resident((1, D_MODEL)),
        ],
        out_specs=[row_block, row_block],
        out_shape=[jax.ShapeDtypeStruct((t, D_MODEL), F32), jax.ShapeDtypeStruct((t, D_MODEL), BF16)],
        compiler_params=_params("parallel"),
        name="mix_out",
    )(y_sb, y_ca, proj2d, proj2d, x2d, w_sb, w_ca, w_mix, g_ffn)


def _ffn_kernel(h_ref, wg_ref, wu_ref, wo_ref, o_ref):
    @pl.when(pl.program_id(1) == 0)
    def _():
        o_ref[...] = jnp.zeros_like(o_ref)

    h = h_ref[...]
    gate = _dot(h, wg_ref[...])
    up = _dot(h, wu_ref[...])
    act = (gate * jax.nn.sigmoid(gate) * up).astype(BF16)
    for c in range(0, D_MODEL, TN_FFN_OUT):
        cols = slice(c, c + TN_FFN_OUT)
        o_ref[:, cols] += _dot(act, wo_ref[:, cols])


def _ffn(h, w_in, w_out):
    t = h.shape[0]
    tm, tf = TM_FFN, TF_FFN
    nf = D_FF // tf
    return pl.pallas_call(
        _ffn_kernel,
        grid=(t // tm, nf),
        in_specs=[
            pl.BlockSpec((tm, D_MODEL), lambda i, f: (i, 0)),
            pl.BlockSpec((D_MODEL, tf), lambda i, f: (0, f)),
            pl.BlockSpec((D_MODEL, tf), lambda i, f: (0, nf + f)),
            pl.BlockSpec((tf, D_MODEL), lambda i, f: (f, 0)),
        ],
        out_specs=pl.BlockSpec((tm, D_MODEL), lambda i, f: (i, 0)),
        out_shape=jax.ShapeDtypeStruct((t, D_MODEL), F32),
        compiler_params=_params("parallel", "arbitrary"),
        name="ffn",
    )(h, w_in, w_in, w_out)


def _ple_kernel(x1_ref, y_ref, p_ref, gp_ref, gf_ref, wgate_ref, wple_ref, o_ref):
    x = x1_ref[...] + y_ref[...]
    h = _rmsnorm_f32(x, gp_ref[...]).astype(BF16)
    gate = jax.nn.sigmoid(_dot(h, wgate_ref[...]))
    emb = _dot(p_ref[...].astype(BF16), wple_ref[...])
    o_ref[...] = _rmsnorm_f32(x + gate * emb, gf_ref[...])


def _ple_final(x1, y_ffn, p_stacked, g_ple, g_final, w_gate, w_ple):
    t = x1.shape[0]
    tm = TM_PLE
    blocks_per_seq = p_stacked.shape[2] // tm
    row_block = pl.BlockSpec((tm, D_MODEL), lambda i: (i, 0))
    return pl.pallas_call(
        _ple_kernel,
        grid=(t // tm,),
        in_specs=[
            row_block,
            row_block,
            pl.BlockSpec((None, None, tm, D_PLE),
                         lambda i: (0, i // blocks_per_seq, i % blocks_per_seq, 0)),
            _resident((1, D_MODEL)),
            _resident((1, D_MODEL)),
            _resident((D_MODEL, D_MODEL)),
            _resident((D_PLE, D_MODEL)),
        ],
        out_specs=row_block,
        out_shape=jax.ShapeDtypeStruct((t, D_MODEL), F32),
        compiler_params=_params("parallel"),
        name="ple_final",
    )(x1, y_ffn, p_stacked, g_ple, g_final, w_gate, w_ple)


def kernel(x, p, w_in, w_sb_out, w_ca_out, w_mix_out, rel_bias, g_mix, g_ffn, g_ple, g_final,
           w_ffn_in, w_ffn_out, w_ple_in, w_ple_gate):
    b, s, d = x.shape
    assert w_in.shape[0] == 1, "the output norm is fused into the (single) layer's last kernel"
    xt = x.reshape(b * s, d)
    proj = _in_proj(xt, g_mix[0][None], w_in[0].astype(BF16))
    proj3d = proj.reshape(b, s, IN_COLS)
    y_sb, (w_sb, w_ca, w_mix, w_ffn_a, w_ffn_b, w_gate, w_ple) = _sb_attention(
        proj3d, [w_sb_out, w_ca_out, w_mix_out, w_ffn_in, w_ffn_out, w_ple_gate, w_ple_in])
    y_sb = y_sb.reshape(b * s, W_ATT)
    y_ca = _ca_attention(proj3d, _ca_bias_table(rel_bias[0])).reshape(b * s, W_ATT)
    x1, h_ffn = _mix_out(y_sb, y_ca, proj, xt, w_sb, w_ca, w_mix, g_ffn[0][None])
    y_ffn = _ffn(h_ffn, w_ffn_a, w_ffn_b)
    out = _ple_final(x1, y_ffn, p, g_ple[0][None], g_final[None],
                     w_gate, w_ple)
    return out.reshape(b, s, d)
```

```python
import functools

import jax
import jax.numpy as jnp
from jax import lax
from jax.experimental import pallas as pl
from jax.experimental.pallas import tpu as pltpu

D_MODEL = 2048
CHUNK = 64
HEAD_DIM = 128
N_HEADS = 8
W_ATT = N_HEADS * HEAD_DIM
LEFT_CHUNKS = 8
REL_CLIP = 128
N_REL = REL_CLIP + CHUNK
D_FF = 5632
D_PLE = 256
EPS = 1e-6
NEG = -1e30
IN_COLS = 6 * W_ATT + 2 * D_MODEL

BF16 = jnp.bfloat16
F32 = jnp.float32

BF16_SUBLANES = 16

VMEM_LIMIT_BYTES = 52 * 1024 * 1024

TM_INPROJ, TN_INPROJ = 1024, 2048
TM_MIX = 256
TM_FFN, TF_FFN, TN_FFN_OUT = 1024, 512, 512
TM_PLE = 512
SB_Q_BLOCK, SB_K_BLOCK = 512, 256
SB_DEAD_LOG_WEIGHT = -110.0
SB_HEADS_PER_STEP = 4
CA_BLOCK = 128
CA_KEY_BLOCKS = LEFT_CHUNKS * CHUNK // CA_BLOCK + 1
CA_GROUP = 2
CA_HEADS_PER_STEP = 4


def _params(*semantics):
    return pltpu.CompilerParams(dimension_semantics=semantics, vmem_limit_bytes=VMEM_LIMIT_BYTES)


def _rmsnorm_f32(x, g):
    ms = jnp.mean(x * x, axis=-1, keepdims=True)
    return x * lax.rsqrt(ms + EPS) * g


def _dot(a, b):
    return jnp.dot(a, b, preferred_element_type=F32)


def _dot_nt(a, b):
    return lax.dot_general(a, b, (((1,), (1,)), ((), ())), preferred_element_type=F32)


def _query_column_scale():
    col = jnp.arange(IN_COLS)[None, :]
    is_query = (col < W_ATT) | ((col >= 3 * W_ATT) & (col < 4 * W_ATT))
    return jnp.where(is_query, HEAD_DIM ** -0.5, 1.0).astype(F32)


def _inproj_kernel(x_ref, g_ref, w_ref, cs_ref, o_ref, h_ref):
    @pl.when(pl.program_id(1) == 0)
    def _():
        h_ref[...] = _rmsnorm_f32(x_ref[...], g_ref[...]).astype(BF16)

    o_ref[...] = (_dot(h_ref[...], w_ref[...]) * cs_ref[...]).astype(o_ref.dtype)


def _in_proj(x2d, g, w_bf16):
    t, d = x2d.shape
    n = w_bf16.shape[1]
    tm, tn = TM_INPROJ, TN_INPROJ
    return pl.pallas_call(
        _inproj_kernel,
        grid=(t // tm, n // tn),
        in_specs=[
            pl.BlockSpec((tm, d), lambda i, j: (i, 0)),
            pl.BlockSpec((1, d), lambda i, j: (0, 0)),
            pl.BlockSpec((d, tn), lambda i, j: (0, j)),
            pl.BlockSpec((1, tn), lambda i, j: (0, j)),
        ],
        out_specs=pl.BlockSpec((tm, tn), lambda i, j: (i, j)),
        out_shape=jax.ShapeDtypeStruct((t, n), BF16),
        scratch_shapes=[pltpu.VMEM((tm, d), BF16)],
        compiler_params=_params("parallel", "arbitrary"),
        name="in_proj",
    )(x2d, g, w_bf16, _query_column_scale())


def _sb_kernel(q_ref, k_ref, v_ref, *refs):
    *refs, acc_ref, carry_ref = refs
    n_cast = len(refs) // 2
    o_ref = refs[n_cast]
    for src_ref, dst_ref in zip(refs[:n_cast], refs[n_cast + 1:]):
        dst_ref[...] = src_ref[...].astype(dst_ref.dtype)

    tq, tk = SB_Q_BLOCK, SB_K_BLOCK
    assert tq == 2 * tk
    heads = range(SB_HEADS_PER_STEP)
    upper, lower, every = slice(0, tk), slice(tk, tq), slice(0, tq)
    i = pl.program_id(2)

    def cols(h):
        return slice(h * HEAD_DIM, (h + 1) * HEAD_DIM)

    row = lax.broadcasted_iota(jnp.int32, (tk, tk), 0)
    col = lax.broadcasted_iota(jnp.int32, (tk, tk), 1)
    suffix = jnp.where(row > col, 1.0, 0.0).astype(BF16)

    def gates(rows, j, diagonal):
        n_rows = rows.stop - rows.start
        start = pl.multiple_of(j * tk, tk)
        zs = [_dot_nt(q_ref[rows, cols(h)], k_ref[pl.ds(start, tk), cols(h)]) for h in heads]
        past = None
        if diagonal:
            past = (lax.broadcasted_iota(jnp.int32, (n_rows, tk), 1)
                    < lax.broadcasted_iota(jnp.int32, (n_rows, tk), 0))
        log_betas, log_keeps = [], []
        for z in zs:
            log_beta = jnp.minimum(z, 0.0) - jnp.log(1.0 + jnp.exp(-jnp.abs(z)))
            log_keep = log_beta - z
            if diagonal:
                log_keep = jnp.where(past, log_keep, 0.0)
            log_betas.append(log_beta)
            log_keeps.append(log_keep)
        sums = [_dot(lk.astype(BF16), suffix) for lk in log_keeps]
        return rows, start, past, log_betas, log_keeps, sums

    def accumulate(staged, first=False):
        rows, start, past, log_betas, log_keeps, sums = staged
        weights = []
        for h, log_beta, between in zip(heads, log_betas, sums):
            a = jnp.exp(log_beta + between if first else log_beta + between + carry_ref[h, rows])
            if past is not None:
                a = jnp.where(past, a, 0.0)
            weights.append(a.astype(BF16))
        for h, a, log_keep in zip(heads, weights, log_keeps):
            contribution = _dot(a, v_ref[pl.ds(start, tk), cols(h)])
            row_sum = jnp.sum(log_keep, axis=1, keepdims=True)
            if first:
                acc_ref[rows, cols(h)] = contribution
                carry_ref[h, rows] = row_sum
            else:
                acc_ref[rows, cols(h)] += contribution
                carry_ref[h, rows] += row_sum

    right = gates(lower, 2 * i + 1, True)
    left = gates(every, 2 * i, True)
    acc_ref[upper, :] = jnp.zeros((tk, acc_ref.shape[1]), F32)
    carry_ref[:, upper] = jnp.zeros((SB_HEADS_PER_STEP, tk, 1), F32)
    accumulate(right, first=True)
    accumulate(left)

    def sweep(rows, watched, j):
        def alive():
            return jnp.max(carry_ref[:, watched]) > SB_DEAD_LOG_WEIGHT

        def cond(state):
            j, live = state
            return jnp.logical_and(j >= 0, live)

        def body(state):
            j, _ = state
            accumulate(gates(rows, j, False))
            return j - 1, alive()

        return lax.while_loop(cond, body, (j, alive()))[0]

    j = sweep(every, lower, 2 * i - 1)
    sweep(upper, upper, j)
    o_ref[...] = acc_ref[...].astype(o_ref.dtype)


def _chunk_rows(rows, max_chunks):
    return next(r for r in range(BF16_SUBLANES, rows + 1, BF16_SUBLANES)
                if rows % r == 0 and rows // r <= max_chunks)


def _sb_attention(proj3d, stacked_weights):
    b, s, _ = proj3d.shape
    blk = SB_Q_BLOCK
    width = SB_HEADS_PER_STEP * HEAD_DIM
    groups = W_ATT // width
    q_blocks = s // blk
    n_steps = b * groups * q_blocks

    in_cast, out_cast, cast_shapes = [], [], []
    for w in stacked_weights:
        _, rows, cols = w.shape
        chunk = _chunk_rows(rows, n_steps)
        last = rows // chunk - 1

        def chunk_index(bi, g, i, last=last):
            return jnp.minimum((bi * groups + g) * q_blocks + i, last)

        in_cast.append(pl.BlockSpec((None, chunk, cols), lambda bi, g, i, f=chunk_index: (0, f(bi, g, i), 0)))
        out_cast.append(pl.BlockSpec((chunk, cols), lambda bi, g, i, f=chunk_index: (f(bi, g, i), 0)))
        cast_shapes.append(jax.ShapeDtypeStruct((rows, cols), BF16))

    y_spec = pl.BlockSpec((None, blk, width), lambda bi, g, i: (bi, i, g))
    y, *cast = pl.pallas_call(
        _sb_kernel,
        grid=(b, groups, q_blocks),
        in_specs=[
            y_spec,
            pl.BlockSpec((None, s, width), lambda bi, g, i: (bi, 0, groups + g)),
            pl.BlockSpec((None, s, width), lambda bi, g, i: (bi, 0, 2 * groups + g)),
        ] + in_cast,
        out_specs=[y_spec] + out_cast,
        out_shape=[jax.ShapeDtypeStruct((b, s, W_ATT), BF16)] + cast_shapes,
        scratch_shapes=[pltpu.VMEM((blk, width), F32), pltpu.VMEM((SB_HEADS_PER_STEP, blk, 1), F32)],
        compiler_params=_params("arbitrary", "arbitrary", "arbitrary"),
        name="sb_attention",
    )(proj3d, proj3d, proj3d, *stacked_weights)
    return y, cast


def _ca_bias_table(rel_bias):
    blk, width = CA_BLOCK, CA_KEY_BLOCKS * CA_BLOCK
    left = LEFT_CHUNKS * CHUNK
    h = rel_bias.shape[0]
    rb = rel_bias.astype(F32)
    n_lo = left + blk - 1 - REL_CLIP
    n_hi = blk - CHUNK + 1
    f = jnp.concatenate([jnp.broadcast_to(rb[:, :1], (h, n_lo)), rb,
                         jnp.broadcast_to(rb[:, -1:], (h, n_hi))], axis=1)
    n = f.shape[1]
    skew = jnp.tile(f, (1, blk))[:, :blk * (n - 1)].reshape(h, blk, n - 1)
    toeplitz = skew[:, :, blk - 1:blk - 1 + width]
    t = jnp.arange(blk)[:, None]
    c = jnp.arange(width)[None, :]
    dchunk = c // CHUNK - LEFT_CHUNKS - t // CHUNK
    valid = (dchunk >= -LEFT_CHUNKS) & (dchunk <= 0)
    return jnp.where(valid[None], toeplitz, NEG)


def _ca_kernel(q_ref, k_ref, v_ref, t_ref, o_ref):
    blk, nb = CA_BLOCK, CA_KEY_BLOCKS

    def rows(i):
        return slice(i * blk, (i + 1) * blk)

    def keys(i):
        return slice(max(0, i - (nb - 1)) * blk, (i + 1) * blk)

    def cols(h):
        return slice(h * HEAD_DIM, (h + 1) * HEAD_DIM)

    def scores(task):
        h, i = task
        nk = i - max(0, i - (nb - 1)) + 1
        return _dot_nt(q_ref[rows(i), cols(h)], k_ref[keys(i), cols(h)]) + t_ref[h, :, (nb - nk) * blk:]

    def softmax_weights(zs):
        ms = [jnp.max(z, axis=1, keepdims=True) for z in zs]
        es = [jnp.exp(z - m) for z, m in zip(zs, ms)]
        return [e.astype(BF16) for e in es], [jnp.sum(e, axis=1, keepdims=True) for e in es]

    def finish(group, es, denoms):
        outs = [_dot(e, v_ref[keys(i), cols(h)]) for e, (h, i) in zip(es, group)]
        for (h, i), out, denom in zip(group, outs, denoms):
            o_ref[rows(i), cols(h)] = (out / denom).astype(o_ref.dtype)

    tasks = [(h, i) for h in range(CA_HEADS_PER_STEP) for i in range(q_ref.shape[0] // blk)]
    groups = [tasks[g:g + CA_GROUP] for g in range(0, len(tasks), CA_GROUP)]
    zs = [scores(task) for task in groups[0]]
    pending = None
    for n, group in enumerate(groups):
        next_zs = [scores(task) for task in groups[n + 1]] if n + 1 < len(groups) else None
        weights = softmax_weights(zs)
        if pending is not None:
            finish(*pending)
        pending = (group, *weights)
        zs = next_zs
    finish(*pending)


def _ca_attention(proj3d, table):
    b, s, _ = proj3d.shape
    width = CA_HEADS_PER_STEP * HEAD_DIM
    groups = W_ATT // width

    def seq_spec(first_group):
        return pl.BlockSpec((None, s, width), lambda bi, g: (bi, 0, first_group + g))

    return pl.pallas_call(
        _ca_kernel,
        grid=(b, groups),
        in_specs=[
            seq_spec(3 * groups),
            seq_spec(4 * groups),
            seq_spec(5 * groups),
            pl.BlockSpec((CA_HEADS_PER_STEP,) + table.shape[1:], lambda bi, g: (g, 0, 0)),
        ],
        out_specs=seq_spec(0),
        out_shape=jax.ShapeDtypeStruct((b, s, W_ATT), BF16),
        compiler_params=_params("parallel", "parallel"),
        name="ca_attention",
    )(proj3d, proj3d, proj3d, table)


def _resident(shape):
    return pl.BlockSpec(shape, lambda *_: (0,) * len(shape), pipeline_mode=pl.Buffered(1))


def _mix_kernel(ysb_ref, yca_ref, gsb_ref, gca_ref, x_ref, wsb_ref, wca_ref, wmix_ref, g_ref,
                x1_ref, h_ref):
    sb = _dot(ysb_ref[...], wsb_ref[...])
    ca = _dot(yca_ref[...], wca_ref[...])
    merged = (jax.nn.sigmoid(gsb_ref[...].astype(F32)) * sb
              + jax.nn.sigmoid(gca_ref[...].astype(F32)) * ca)
    x1 = x_ref[...] + _dot(merged.astype(BF16), wmix_ref[...])
    x1_ref[...] = x1
    h_ref[...] = _rmsnorm_f32(x1, g_ref[...]).astype(BF16)


def _mix_out(y_sb, y_ca, proj2d, x2d, w_sb, w_ca, w_mix, g_ffn):
    t = x2d.shape[0]
    tm = TM_MIX
    gate_blk = 6 * W_ATT // D_MODEL
    att_block = pl.BlockSpec((tm, W_ATT), lambda i: (i, 0))
    row_block = pl.BlockSpec((tm, D_MODEL), lambda i: (i, 0))
    return pl.pallas_call(
        _mix_kernel,
        grid=(t // tm,),
        in_specs=[
            att_block,
            att_block,
            pl.BlockSpec((tm, D_MODEL), lambda i: (i, gate_blk)),
            pl.BlockSpec((tm, D_MODEL), lambda i: (i, gate_blk + 1)),
            row_block,
            _resident((W_ATT, D_MODEL)),
            _resident((W_ATT, D_MODEL)),
            _resident((D_MODEL, D_MODEL)),
            _resident((1, D_MODEL)),
        ],
        out_specs=[row_block, row_block],
        out_shape=[jax.ShapeDtypeStruct((t, D_MODEL), F32), jax.ShapeDtypeStruct((t, D_MODEL), BF16)],
        compiler_params=_params("parallel"),
        name="mix_out",
    )(y_sb, y_ca, proj2d, proj2d, x2d, w_sb, w_ca, w_mix, g_ffn)


def _ffn_kernel(h_ref, wg_ref, wu_ref, wo_ref, o_ref):
    @pl.when(pl.program_id(1) == 0)
    def _():
        o_ref[...] = jnp.zeros_like(o_ref)

    h = h_ref[...]
    gate = _dot(h, wg_ref[...])
    up = _dot(h, wu_ref[...])
    act = (gate * jax.nn.sigmoid(gate) * up).astype(BF16)
    for c in range(0, D_MODEL, TN_FFN_OUT):
        cols = slice(c, c + TN_FFN_OUT)
        o_ref[:, cols] += _dot(act, wo_ref[:, cols])


def _ffn(h, w_in, w_out):
    t = h.shape[0]
    tm, tf = TM_FFN, TF_FFN
    nf = D_FF // tf
    return pl.pallas_call(
        _ffn_kernel,
        grid=(t // tm, nf),
        in_specs=[
            pl.BlockSpec((tm, D_MODEL), lambda i, f: (i, 0)),
            pl.BlockSpec((D_MODEL, tf), lambda i, f: (0, f)),
            pl.BlockSpec((D_MODEL, tf), lambda i, f: (0, nf + f)),
            pl.BlockSpec((tf, D_MODEL), lambda i, f: (f, 0)),
        ],
        out_specs=pl.BlockSpec((tm, D_MODEL), lambda i, f: (i, 0)),
        out_shape=jax.ShapeDtypeStruct((t, D_MODEL), F32),
        compiler_params=_params("parallel", "arbitrary"),
        name="ffn",
    )(h, w_in, w_in, w_out)


def _ple_kernel(x1_ref, y_ref, p_ref, gp_ref, gf_ref, wgate_ref, wple_ref, o_ref):
    x = x1_ref[...] + y_ref[...]
    h = _rmsnorm_f32(x, gp_ref[...]).astype(BF16)
    gate = jax.nn.sigmoid(_dot(h, wgate_ref[...]))
    emb = _dot(p_ref[...].astype(BF16), wple_ref[...])
    o_ref[...] = _rmsnorm_f32(x + gate * emb, gf_ref[...])


def _ple_final(x1, y_ffn, p_stacked, g_ple, g_final, w_gate, w_ple):
    t = x1.shape[0]
    tm = TM_PLE
    blocks_per_seq = p_stacked.shape[2] // tm
    row_block = pl.BlockSpec((tm, D_MODEL), lambda i: (i, 0))
    return pl.pallas_call(
        _ple_kernel,
        grid=(t // tm,),
        in_specs=[
            row_block,
            row_block,
            pl.BlockSpec((None, None, tm, D_PLE),
                         lambda i: (0, i // blocks_per_seq, i % blocks_per_seq, 0)),
            _resident((1, D_MODEL)),
            _resident((1, D_MODEL)),
            _resident((D_MODEL, D_MODEL)),
            _resident((D_PLE, D_MODEL)),
        ],
        out_specs=row_block,
        out_shape=jax.ShapeDtypeStruct((t, D_MODEL), F32),
        compiler_params=_params("parallel"),
        name="ple_final",
    )(x1, y_ffn, p_stacked, g_ple, g_final, w_gate, w_ple)


def kernel(x, p, w_in, w_sb_out, w_ca_out, w_mix_out, rel_bias, g_mix, g_ffn, g_ple, g_final,
           w_ffn_in, w_ffn_out, w_ple_in, w_ple_gate):
    b, s, d = x.shape
    assert w_in.shape[0] == 1, "the output norm is fused into the (single) layer's last kernel"
    xt = x.reshape(b * s, d)
    proj = _in_proj(xt, g_mix[0][None], w_in[0].astype(BF16))
    proj3d = proj.reshape(b, s, IN_COLS)
    y_sb, (w_sb, w_ca, w_mix, w_ffn_a, w_ffn_b, w_gate, w_ple) = _sb_attention(
        proj3d, [w_sb_out, w_ca_out, w_mix_out, w_ffn_in, w_ffn_out, w_ple_gate, w_ple_in])
    y_sb = y_sb.reshape(b * s, W_ATT)
    y_ca = _ca_attention(proj3d, _ca_bias_table(rel_bias[0])).reshape(b * s, W_ATT)
    x1, h_ffn = _mix_out(y_sb, y_ca, proj, xt, w_sb, w_ca, w_mix, g_ffn[0][None])
    y_ffn = _ffn(h_ffn, w_ffn_a, w_ffn_b)
    out = _ple_final(x1, y_ffn, p, g_ple[0][None], g_final[None],
                     w_gate, w_ple)
    return out.reshape(b, s, d)
```

```python
import functools

import jax
import jax.numpy as jnp
from jax import lax
from jax.experimental import pallas as pl
from jax.experimental.pallas import tpu as pltpu

D_MODEL = 2048
CHUNK = 64
HEAD_DIM = 128
N_HEADS = 8
W_ATT = N_HEADS * HEAD_DIM
LEFT_CHUNKS = 8
REL_CLIP = 128
N_REL = REL_CLIP + CHUNK
D_FF = 5632
D_PLE = 256
EPS = 1e-6
NEG = -1e30
IN_COLS = 6 * W_ATT + 2 * D_MODEL

BF16 = jnp.bfloat16
F32 = jnp.float32

BF16_SUBLANES = 16

VMEM_LIMIT_BYTES = 52 * 1024 * 1024

TM_INPROJ, TN_INPROJ = 1024, 2048
TM_MIX = 256
TM_FFN, TF_FFN, FFN_PIECE = 1024, 512, 256
TM_PLE = 512
SB_Q_BLOCK, SB_K_BLOCK = 512, 256
SB_DEAD_LOG_WEIGHT = -110.0
SB_HEADS_PER_STEP = 4
CA_BLOCK = 128
CA_KEY_BLOCKS = LEFT_CHUNKS * CHUNK // CA_BLOCK + 1
CA_GROUP = 2
CA_HEADS_PER_STEP = 4


def _params(*semantics):
    return pltpu.CompilerParams(dimension_semantics=semantics, vmem_limit_bytes=VMEM_LIMIT_BYTES)


def _rmsnorm_f32(x, g):
    ms = jnp.mean(x * x, axis=-1, keepdims=True)
    return x * lax.rsqrt(ms + EPS) * g


def _dot(a, b):
    return jnp.dot(a, b, preferred_element_type=F32)


def _dot_nt(a, b):
    return lax.dot_general(a, b, (((1,), (1,)), ((), ())), preferred_element_type=F32)


def _query_column_scale():
    col = jnp.arange(IN_COLS)[None, :]
    is_query = (col < W_ATT) | ((col >= 3 * W_ATT) & (col < 4 * W_ATT))
    return jnp.where(is_query, HEAD_DIM ** -0.5, 1.0).astype(F32)


def _inproj_kernel(x_ref, g_ref, w_ref, cs_ref, o_ref, h_ref):
    @pl.when(pl.program_id(1) == 0)
    def _():
        h_ref[...] = _rmsnorm_f32(x_ref[...], g_ref[...]).astype(BF16)

    o_ref[...] = (_dot(h_ref[...], w_ref[...]) * cs_ref[...]).astype(o_ref.dtype)


def _in_proj(x2d, g, w_bf16):
    t, d = x2d.shape
    n = w_bf16.shape[1]
    tm, tn = TM_INPROJ, TN_INPROJ
    return pl.pallas_call(
        _inproj_kernel,
        grid=(t // tm, n // tn),
        in_specs=[
            pl.BlockSpec((tm, d), lambda i, j: (i, 0)),
            pl.BlockSpec((1, d), lambda i, j: (0, 0)),
            pl.BlockSpec((d, tn), lambda i, j: (0, j)),
            pl.BlockSpec((1, tn), lambda i, j: (0, j)),
        ],
        out_specs=pl.BlockSpec((tm, tn), lambda i, j: (i, j)),
        out_shape=jax.ShapeDtypeStruct((t, n), BF16),
        scratch_shapes=[pltpu.VMEM((tm, d), BF16)],
        compiler_params=_params("parallel", "arbitrary"),
        name="in_proj",
    )(x2d, g, w_bf16, _query_column_scale())


def _sb_kernel(q_ref, k_ref, v_ref, *refs):
    *refs, acc_ref, carry_ref = refs
    n_cast = len(refs) // 2
    o_ref = refs[n_cast]
    for src_ref, dst_ref in zip(refs[:n_cast], refs[n_cast + 1:]):
        dst_ref[...] = src_ref[...].astype(dst_ref.dtype)

    tq, tk = SB_Q_BLOCK, SB_K_BLOCK
    assert tq == 2 * tk
    heads = range(SB_HEADS_PER_STEP)
    upper, lower, every = slice(0, tk), slice(tk, tq), slice(0, tq)
    i = pl.program_id(2)

    def cols(h):
        return slice(h * HEAD_DIM, (h + 1) * HEAD_DIM)

    row = lax.broadcasted_iota(jnp.int32, (tk, tk), 0)
    col = lax.broadcasted_iota(jnp.int32, (tk, tk), 1)
    suffix = jnp.where(row > col, 1.0, 0.0).astype(BF16)

    def gates(rows, j, diagonal):
        n_rows = rows.stop - rows.start
        start = pl.multiple_of(j * tk, tk)
        zs = [_dot_nt(q_ref[rows, cols(h)], k_ref[pl.ds(start, tk), cols(h)]) for h in heads]
        past = None
        if diagonal:
            past = (lax.broadcasted_iota(jnp.int32, (n_rows, tk), 1)
                    < lax.broadcasted_iota(jnp.int32, (n_rows, tk), 0))
        log_betas, log_keeps = [], []
        for z in zs:
            log_beta = jnp.minimum(z, 0.0) - jnp.log(1.0 + jnp.exp(-jnp.abs(z)))
            log_keep = log_beta - z
            if diagonal:
                log_keep = jnp.where(past, log_keep, 0.0)
            log_betas.append(log_beta)
            log_keeps.append(log_keep)
        sums = [_dot(lk.astype(BF16), suffix) for lk in log_keeps]
        return rows, start, past, log_betas, log_keeps, sums

    def accumulate(staged, first=False):
        rows, start, past, log_betas, log_keeps, sums = staged
        weights = []
        for h, log_beta, between in zip(heads, log_betas, sums):
            a = jnp.exp(log_beta + between if first else log_beta + between + carry_ref[h, rows])
            if past is not None:
                a = jnp.where(past, a, 0.0)
            weights.append(a.astype(BF16))
        for h, a, log_keep in zip(heads, weights, log_keeps):
            contribution = _dot(a, v_ref[pl.ds(start, tk), cols(h)])
            row_sum = jnp.sum(log_keep, axis=1, keepdims=True)
            if first:
                acc_ref[rows, cols(h)] = contribution
                carry_ref[h, rows] = row_sum
            else:
                acc_ref[rows, cols(h)] += contribution
                carry_ref[h, rows] += row_sum

    right = gates(lower, 2 * i + 1, True)
    left = gates(every, 2 * i, True)
    acc_ref[upper, :] = jnp.zeros((tk, acc_ref.shape[1]), F32)
    carry_ref[:, upper] = jnp.zeros((SB_HEADS_PER_STEP, tk, 1), F32)
    accumulate(right, first=True)
    accumulate(left)

    def sweep(rows, watched, j):
        def alive():
            return jnp.max(carry_ref[:, watched]) > SB_DEAD_LOG_WEIGHT

        def cond(state):
            j, live = state
            return jnp.logical_and(j >= 0, live)

        def body(state):
            j, _ = state
            accumulate(gates(rows, j, False))
            return j - 1, alive()

        return lax.while_loop(cond, body, (j, alive()))[0]

    j = sweep(every, lower, 2 * i - 1)
    sweep(upper, upper, j)
    o_ref[...] = acc_ref[...].astype(o_ref.dtype)


def _chunk_rows(rows, max_chunks):
    return next(r for r in range(BF16_SUBLANES, rows + 1, BF16_SUBLANES)
                if rows % r == 0 and rows // r <= max_chunks)


def _sb_attention(proj3d, stacked_weights):
    b, s, _ = proj3d.shape
    blk = SB_Q_BLOCK
    width = SB_HEADS_PER_STEP * HEAD_DIM
    groups = W_ATT // width
    q_blocks = s // blk
    n_steps = b * groups * q_blocks

    in_cast, out_cast, cast_shapes = [], [], []
    for w in stacked_weights:
        _, rows, cols = w.shape
        chunk = _chunk_rows(rows, n_steps)
        last = rows // chunk - 1

        def chunk_index(bi, g, i, last=last):
            return jnp.minimum((bi * groups + g) * q_blocks + i, last)

        in_cast.append(pl.BlockSpec((None, chunk, cols), lambda bi, g, i, f=chunk_index: (0, f(bi, g, i), 0)))
        out_cast.append(pl.BlockSpec((chunk, cols), lambda bi, g, i, f=chunk_index: (f(bi, g, i), 0)))
        cast_shapes.append(jax.ShapeDtypeStruct((rows, cols), BF16))

    y_spec = pl.BlockSpec((None, blk, width), lambda bi, g, i: (bi, i, g))
    y, *cast = pl.pallas_call(
        _sb_kernel,
        grid=(b, groups, q_blocks),
        in_specs=[
            y_spec,
            pl.BlockSpec((None, s, width), lambda bi, g, i: (bi, 0, groups + g)),
            pl.BlockSpec((None, s, width), lambda bi, g, i: (bi, 0, 2 * groups + g)),
        ] + in_cast,
        out_specs=[y_spec] + out_cast,
        out_shape=[jax.ShapeDtypeStruct((b, s, W_ATT), BF16)] + cast_shapes,
        scratch_shapes=[pltpu.VMEM((blk, width), F32), pltpu.VMEM((SB_HEADS_PER_STEP, blk, 1), F32)],
        compiler_params=_params("arbitrary", "arbitrary", "arbitrary"),
        name="sb_attention",
    )(proj3d, proj3d, proj3d, *stacked_weights)
    return y, cast


def _ca_bias_table(rel_bias):
    blk, width = CA_BLOCK, CA_KEY_BLOCKS * CA_BLOCK
    left = LEFT_CHUNKS * CHUNK
    h = rel_bias.shape[0]
    rb = rel_bias.astype(F32)
    n_lo = left + blk - 1 - REL_CLIP
    n_hi = blk - CHUNK + 1
    f = jnp.concatenate([jnp.broadcast_to(rb[:, :1], (h, n_lo)), rb,
                         jnp.broadcast_to(rb[:, -1:], (h, n_hi))], axis=1)
    n = f.shape[1]
    skew = jnp.tile(f, (1, blk))[:, :blk * (n - 1)].reshape(h, blk, n - 1)
    toeplitz = skew[:, :, blk - 1:blk - 1 + width]
    t = jnp.arange(blk)[:, None]
    c = jnp.arange(width)[None, :]
    dchunk = c // CHUNK - LEFT_CHUNKS - t // CHUNK
    valid = (dchunk >= -LEFT_CHUNKS) & (dchunk <= 0)
    return jnp.where(valid[None], toeplitz, NEG)


def _ca_kernel(q_ref, k_ref, v_ref, t_ref, o_ref):
    blk, nb = CA_BLOCK, CA_KEY_BLOCKS

    def rows(i):
        return slice(i * blk, (i + 1) * blk)

    def keys(i):
        return slice(max(0, i - (nb - 1)) * blk, (i + 1) * blk)

    def cols(h):
        return slice(h * HEAD_DIM, (h + 1) * HEAD_DIM)

    def scores(task):
        h, i = task
        nk = i - max(0, i - (nb - 1)) + 1
        return _dot_nt(q_ref[rows(i), cols(h)], k_ref[keys(i), cols(h)]) + t_ref[h, :, (nb - nk) * blk:]

    def softmax_weights(zs):
        ms = [jnp.max(z, axis=1, keepdims=True) for z in zs]
        es = [jnp.exp(z - m) for z, m in zip(zs, ms)]
        return [e.astype(BF16) for e in es], [jnp.sum(e, axis=1, keepdims=True) for e in es]

    def finish(group, es, denoms):
        outs = [_dot(e, v_ref[keys(i), cols(h)]) for e, (h, i) in zip(es, group)]
        for (h, i), out, denom in zip(group, outs, denoms):
            o_ref[rows(i), cols(h)] = (out / denom).astype(o_ref.dtype)

    tasks = [(h, i) for h in range(CA_HEADS_PER_STEP) for i in range(q_ref.shape[0] // blk)]
    groups = [tasks[g:g + CA_GROUP] for g in range(0, len(tasks), CA_GROUP)]
    zs = [scores(task) for task in groups[0]]
    pending = None
    for n, group in enumerate(groups):
        next_zs = [scores(task) for task in groups[n + 1]] if n + 1 < len(groups) else None
        weights = softmax_weights(zs)
        if pending is not None:
            finish(*pending)
        pending = (group, *weights)
        zs = next_zs
    finish(*pending)


def _ca_attention(proj3d, table):
    b, s, _ = proj3d.shape
    width = CA_HEADS_PER_STEP * HEAD_DIM
    groups = W_ATT // width

    def seq_spec(first_group):
        return pl.BlockSpec((None, s, width), lambda bi, g: (bi, 0, first_group + g))

    return pl.pallas_call(
        _ca_kernel,
        grid=(b, groups),
        in_specs=[
            seq_spec(3 * groups),
            seq_spec(4 * groups),
            seq_spec(5 * groups),
            pl.BlockSpec((CA_HEADS_PER_STEP,) + table.shape[1:], lambda bi, g: (g, 0, 0)),
        ],
        out_specs=seq_spec(0),
        out_shape=jax.ShapeDtypeStruct((b, s, W_ATT), BF16),
        compiler_params=_params("parallel", "parallel"),
        name="ca_attention",
    )(proj3d, proj3d, proj3d, table)


def _resident(shape):
    return pl.BlockSpec(shape, lambda *_: (0,) * len(shape), pipeline_mode=pl.Buffered(1))


def _mix_kernel(ysb_ref, yca_ref, gsb_ref, gca_ref, x_ref, wsb_ref, wca_ref, wmix_ref, g_ref,
                x1_ref, h_ref):
    sb = _dot(ysb_ref[...], wsb_ref[...])
    ca = _dot(yca_ref[...], wca_ref[...])
    merged = (jax.nn.sigmoid(gsb_ref[...].astype(F32)) * sb
              + jax.nn.sigmoid(gca_ref[...].astype(F32)) * ca)
    x1 = x_ref[...] + _dot(merged.astype(BF16), wmix_ref[...])
    x1_ref[...] = x1
    h_ref[...] = _rmsnorm_f32(x1, g_ref[...]).astype(BF16)


def _mix_out(y_sb, y_ca, proj2d, x2d, w_sb, w_ca, w_mix, g_ffn):
    t = x2d.shape[0]
    tm = TM_MIX
    gate_blk = 6 * W_ATT // D_MODEL
    att_block = pl.BlockSpec((tm, W_ATT), lambda i: (i, 0))
    row_block = pl.BlockSpec((tm, D_MODEL), lambda i: (i, 0))
    return pl.pallas_call(
        _mix_kernel,
        grid=(t // tm,),
        in_specs=[
            att_block,
            att_block,
            pl.BlockSpec((tm, D_MODEL), lambda i: (i, gate_blk)),
            pl.BlockSpec((tm, D_MODEL), lambda i: (i, gate_blk + 1)),
            row_block,
            _resident((W_ATT, D_MODEL)),
            _resident((W_ATT, D_MODEL)),
            _resident((D_MODEL, D_MODEL)),
            _resident((1, D_MODEL)),
        ],
        out_specs=[row_block, row_block],
        out_shape=[jax.ShapeDtypeStruct((t, D_MODEL), F32), jax.ShapeDtypeStruct((t, D_MODEL), BF16)],
        compiler_params=_params("parallel"),
        name="mix_out",
    )(y_sb, y_ca, proj2d, proj2d, x2d, w_sb, w_ca, w_mix, g_ffn)


def _ffn_kernel(h_ref, wg_ref, wu_ref, wo_ref, o_ref):
    @pl.when(pl.program_id(1) == 0)
    def _():
        o_ref[...] = jnp.zeros_like(o_ref)

    h = h_ref[...]
    acts = []
    for p in range(0, wg_ref.shape[1], FFN_PIECE):
        gate = _dot(h, wg_ref[:, p:p + FFN_PIECE])
        up = _dot(h, wu_ref[:, p:p + FFN_PIECE])
        acts.append((gate * jax.nn.sigmoid(gate) * up).astype(BF16))
    o_ref[...] += _dot(jnp.concatenate(acts, axis=1), wo_ref[...])


def _ffn(h, w_in, w_out):
    t = h.shape[0]
    tm, tf = TM_FFN, TF_FFN
    nf = D_FF // tf
    return pl.pallas_call(
        _ffn_kernel,
        grid=(t // tm, nf),
        in_specs=[
            pl.BlockSpec((tm, D_MODEL), lambda i, f: (i, 0)),
            pl.BlockSpec((D_MODEL, tf), lambda i, f: (0, f)),
            pl.BlockSpec((D_MODEL, tf), lambda i, f: (0, nf + f)),
            pl.BlockSpec((tf, D_MODEL), lambda i, f: (f, 0)),
        ],
        out_specs=pl.BlockSpec((tm, D_MODEL), lambda i, f: (i, 0)),
        out_shape=jax.ShapeDtypeStruct((t, D_MODEL), F32),
        compiler_params=_params("parallel", "arbitrary"),
        name="ffn",
    )(h, w_in, w_in, w_out)


def _ple_kernel(x1_ref, y_ref, p_ref, gp_ref, gf_ref, wgate_ref, wple_ref, o_ref):
    x = x1_ref[...] + y_ref[...]
    h = _rmsnorm_f32(x, gp_ref[...]).astype(BF16)
    gate = jax.nn.sigmoid(_dot(h, wgate_ref[...]))
    emb = _dot(p_ref[...].astype(BF16), wple_ref[...])
    o_ref[...] = _rmsnorm_f32(x + gate * emb, gf_ref[...])


def _ple_final(x1, y_ffn, p_stacked, g_ple, g_final, w_gate, w_ple):
    t = x1.shape[0]
    tm = TM_PLE
    blocks_per_seq = p_stacked.shape[2] // tm
    row_block = pl.BlockSpec((tm, D_MODEL), lambda i: (i, 0))
    return pl.pallas_call(
        _ple_kernel,
        grid=(t // tm,),
        in_specs=[
            row_block,
            row_block,
            pl.BlockSpec((None, None, tm, D_PLE),
                         lambda i: (0, i // blocks_per_seq, i % blocks_per_seq, 0)),
            _resident((1, D_MODEL)),
            _resident((1, D_MODEL)),
            _resident((D_MODEL, D_MODEL)),
            _resident((D_PLE, D_MODEL)),
        ],
        out_specs=row_block,
        out_shape=jax.ShapeDtypeStruct((t, D_MODEL), F32),
        compiler_params=_params("parallel"),
        name="ple_final",
    )(x1, y_ffn, p_stacked, g_ple, g_final, w_gate, w_ple)


def kernel(x, p, w_in, w_sb_out, w_ca_out, w_mix_out, rel_bias, g_mix, g_ffn, g_ple, g_final,
           w_ffn_in, w_ffn_out, w_ple_in, w_ple_gate):
    b, s, d = x.shape
    assert w_in.shape[0] == 1, "the output norm is fused into the (single) layer's last kernel"
    xt = x.reshape(b * s, d)
    proj = _in_proj(xt, g_mix[0][None], w_in[0].astype(BF16))
    proj3d = proj.reshape(b, s, IN_COLS)
    y_sb, (w_sb, w_ca, w_mix, w_ffn_a, w_ffn_b, w_gate, w_ple) = _sb_attention(
        proj3d, [w_sb_out, w_ca_out, w_mix_out, w_ffn_in, w_ffn_out, w_ple_gate, w_ple_in])
    y_sb = y_sb.reshape(b * s, W_ATT)
    y_ca = _ca_attention(proj3d, _ca_bias_table(rel_bias[0])).reshape(b * s, W_ATT)
    x1, h_ffn = _mix_out(y_sb, y_ca, proj, xt, w_sb, w_ca, w_mix, g_ffn[0][None])
    y_ffn = _ffn(h_ffn, w_ffn_a, w_ffn_b)
    out = _ple_final(x1, y_ffn, p, g_ple[0][None], g_final[None],
                     w_gate, w_ple)
    return out.reshape(b, s, d)
```

```python
import functools

import jax
import jax.numpy as jnp
from jax import lax
from jax.experimental import pallas as pl
from jax.experimental.pallas import tpu as pltpu

D_MODEL = 2048
CHUNK = 64
HEAD_DIM = 128
N_HEADS = 8
W_ATT = N_HEADS * HEAD_DIM
LEFT_CHUNKS = 8
REL_CLIP = 128
N_REL = REL_CLIP + CHUNK
D_FF = 5632
D_PLE = 256
EPS = 1e-6
NEG = -1e30
IN_COLS = 6 * W_ATT + 2 * D_MODEL

BF16 = jnp.bfloat16
F32 = jnp.float32

BF16_SUBLANES = 16

VMEM_LIMIT_BYTES = 52 * 1024 * 1024

TM_INPROJ, TN_INPROJ = 1024, 2048
TM_MIX = 256
TM_FFN, TF_FFN, FFN_PIECE = 1024, 512, 256
TM_PLE = 512
SB_Q_BLOCK, SB_K_BLOCK = 512, 256
SB_DEAD_LOG_WEIGHT = -110.0
SB_HEADS_PER_STEP = 4
CA_BLOCK = 128
CA_KEY_BLOCKS = LEFT_CHUNKS * CHUNK // CA_BLOCK + 1
CA_GROUP = 2
CA_HEADS_PER_STEP = 4


def _params(*semantics):
    return pltpu.CompilerParams(dimension_semantics=semantics, vmem_limit_bytes=VMEM_LIMIT_BYTES)


def _rmsnorm_f32(x, g):
    ms = jnp.mean(x * x, axis=-1, keepdims=True)
    return x * lax.rsqrt(ms + EPS) * g


def _dot(a, b):
    return jnp.dot(a, b, preferred_element_type=F32)


def _dot_nt(a, b):
    return lax.dot_general(a, b, (((1,), (1,)), ((), ())), preferred_element_type=F32)


def _query_column_scale():
    col = jnp.arange(IN_COLS)[None, :]
    is_query = (col < W_ATT) | ((col >= 3 * W_ATT) & (col < 4 * W_ATT))
    return jnp.where(is_query, HEAD_DIM ** -0.5, 1.0).astype(F32)


def _inproj_kernel(x_ref, g_ref, w_ref, cs_ref, o_ref, h_ref):
    @pl.when(pl.program_id(1) == 0)
    def _():
        h_ref[...] = _rmsnorm_f32(x_ref[...], g_ref[...]).astype(BF16)

    o_ref[...] = (_dot(h_ref[...], w_ref[...]) * cs_ref[...]).astype(o_ref.dtype)


def _in_proj(x2d, g, w_bf16):
    t, d = x2d.shape
    n = w_bf16.shape[1]
    tm, tn = TM_INPROJ, TN_INPROJ
    return pl.pallas_call(
        _inproj_kernel,
        grid=(t // tm, n // tn),
        in_specs=[
            pl.BlockSpec((tm, d), lambda i, j: (i, 0)),
            pl.BlockSpec((1, d), lambda i, j: (0, 0)),
            pl.BlockSpec((d, tn), lambda i, j: (0, j)),
            pl.BlockSpec((1, tn), lambda i, j: (0, j)),
        ],
        out_specs=pl.BlockSpec((tm, tn), lambda i, j: (i, j)),
        out_shape=jax.ShapeDtypeStruct((t, n), BF16),
        scratch_shapes=[pltpu.VMEM((tm, d), BF16)],
        compiler_params=_params("parallel", "arbitrary"),
        name="in_proj",
    )(x2d, g, w_bf16, _query_column_scale())


def _sb_kernel(q_ref, k_ref, v_ref, *refs):
    *refs, acc_ref, carry_ref = refs
    n_cast = len(refs) // 2
    o_ref = refs[n_cast]
    for src_ref, dst_ref in zip(refs[:n_cast], refs[n_cast + 1:]):
        dst_ref[...] = src_ref[...].astype(dst_ref.dtype)

    tq, tk = SB_Q_BLOCK, SB_K_BLOCK
    assert tq == 2 * tk
    heads = range(SB_HEADS_PER_STEP)
    upper, lower, every = slice(0, tk), slice(tk, tq), slice(0, tq)
    i = pl.program_id(2)

    def cols(h):
        return slice(h * HEAD_DIM, (h + 1) * HEAD_DIM)

    row = lax.broadcasted_iota(jnp.int32, (tk, tk), 0)
    col = lax.broadcasted_iota(jnp.int32, (tk, tk), 1)
    suffix = jnp.where(row > col, 1.0, 0.0).astype(BF16)

    def gates(rows, j, diagonal):
        n_rows = rows.stop - rows.start
        start = pl.multiple_of(j * tk, tk)
        zs = [_dot_nt(q_ref[rows, cols(h)], k_ref[pl.ds(start, tk), cols(h)]) for h in heads]
        past = None
        if diagonal:
            past = (lax.broadcasted_iota(jnp.int32, (n_rows, tk), 1)
                    < lax.broadcasted_iota(jnp.int32, (n_rows, tk), 0))
        log_betas, log_keeps = [], []
        for z in zs:
            log_beta = jnp.minimum(z, 0.0) - jnp.log(1.0 + jnp.exp(-jnp.abs(z)))
            log_keep = log_beta - z
            if diagonal:
                log_keep = jnp.where(past, log_keep, 0.0)
            log_betas.append(log_beta)
            log_keeps.append(log_keep)
        sums = [_dot(lk.astype(BF16), suffix) for lk in log_keeps]
        return rows, start, past, log_betas, log_keeps, sums

    def accumulate(staged, first=False):
        rows, start, past, log_betas, log_keeps, sums = staged
        weights = []
        for h, log_beta, between in zip(heads, log_betas, sums):
            a = jnp.exp(log_beta + between if first else log_beta + between + carry_ref[h, rows])
            if past is not None:
                a = jnp.where(past, a, 0.0)
            weights.append(a.astype(BF16))
        for h, a, log_keep in zip(heads, weights, log_keeps):
            contribution = _dot(a, v_ref[pl.ds(start, tk), cols(h)])
            row_sum = jnp.sum(log_keep, axis=1, keepdims=True)
            if first:
                acc_ref[rows, cols(h)] = contribution
                carry_ref[h, rows] = row_sum
            else:
                acc_ref[rows, cols(h)] += contribution
                carry_ref[h, rows] += row_sum

    right = gates(lower, 2 * i + 1, True)
    left_upper = gates(upper, 2 * i, True)
    left_lower = gates(lower, 2 * i, False)
    accumulate(right, first=True)
    accumulate(left_upper, first=True)
    accumulate(left_lower)

    def sweep(rows, watched, j):
        def alive():
            return jnp.max(carry_ref[:, watched]) > SB_DEAD_LOG_WEIGHT

        def cond(state):
            j, live = state
            return jnp.logical_and(j >= 0, live)

        def body(state):
            j, _ = state
            accumulate(gates(rows, j, False))
            return j - 1, alive()

        return lax.while_loop(cond, body, (j, alive()))[0]

    j = sweep(every, lower, 2 * i - 1)
    sweep(upper, upper, j)
    o_ref[...] = acc_ref[...].astype(o_ref.dtype)


def _chunk_rows(rows, max_chunks):
    return next(r for r in range(BF16_SUBLANES, rows + 1, BF16_SUBLANES)
                if rows % r == 0 and rows // r <= max_chunks)


def _sb_attention(proj3d, stacked_weights):
    b, s, _ = proj3d.shape
    blk = SB_Q_BLOCK
    width = SB_HEADS_PER_STEP * HEAD_DIM
    groups = W_ATT // width
    q_blocks = s // blk
    n_steps = b * groups * q_blocks

    in_cast, out_cast, cast_shapes = [], [], []
    for w in stacked_weights:
        _, rows, cols = w.shape
        chunk = _chunk_rows(rows, n_steps)
        last = rows // chunk - 1

        def chunk_index(bi, g, i, last=last):
            return jnp.minimum((bi * groups + g) * q_blocks + i, last)

        in_cast.append(pl.BlockSpec((None, chunk, cols), lambda bi, g, i, f=chunk_index: (0, f(bi, g, i), 0)))
        out_cast.append(pl.BlockSpec((chunk, cols), lambda bi, g, i, f=chunk_index: (f(bi, g, i), 0)))
        cast_shapes.append(jax.ShapeDtypeStruct((rows, cols), BF16))

    y_spec = pl.BlockSpec((None, blk, width), lambda bi, g, i: (bi, i, g))
    y, *cast = pl.pallas_call(
        _sb_kernel,
        grid=(b, groups, q_blocks),
        in_specs=[
            y_spec,
            pl.BlockSpec((None, s, width), lambda bi, g, i: (bi, 0, groups + g)),
            pl.BlockSpec((None, s, width), lambda bi, g, i: (bi, 0, 2 * groups + g)),
        ] + in_cast,
        out_specs=[y_spec] + out_cast,
        out_shape=[jax.ShapeDtypeStruct((b, s, W_ATT), BF16)] + cast_shapes,
        scratch_shapes=[pltpu.VMEM((blk, width), F32), pltpu.VMEM((SB_HEADS_PER_STEP, blk, 1), F32)],
        compiler_params=_params("arbitrary", "arbitrary", "arbitrary"),
        name="sb_attention",
    )(proj3d, proj3d, proj3d, *stacked_weights)
    return y, cast


def _ca_bias_table(rel_bias):
    blk, width = CA_BLOCK, CA_KEY_BLOCKS * CA_BLOCK
    left = LEFT_CHUNKS * CHUNK
    h = rel_bias.shape[0]
    rb = rel_bias.astype(F32)
    n_lo = left + blk - 1 - REL_CLIP
    n_hi = blk - CHUNK + 1
    f = jnp.concatenate([jnp.broadcast_to(rb[:, :1], (h, n_lo)), rb,
                         jnp.broadcast_to(rb[:, -1:], (h, n_hi))], axis=1)
    n = f.shape[1]
    skew = jnp.tile(f, (1, blk))[:, :blk * (n - 1)].reshape(h, blk, n - 1)
    toeplitz = skew[:, :, blk - 1:blk - 1 + width]
    t = jnp.arange(blk)[:, None]
    c = jnp.arange(width)[None, :]
    dchunk = c // CHUNK - LEFT_CHUNKS - t // CHUNK
    valid = (dchunk >= -LEFT_CHUNKS) & (dchunk <= 0)
    return jnp.where(valid[None], toeplitz, NEG)


def _ca_kernel(q_ref, k_ref, v_ref, t_ref, o_ref):
    blk, nb = CA_BLOCK, CA_KEY_BLOCKS

    def rows(i):
        return slice(i * blk, (i + 1) * blk)

    def keys(i):
        return slice(max(0, i - (nb - 1)) * blk, (i + 1) * blk)

    def cols(h):
        return slice(h * HEAD_DIM, (h + 1) * HEAD_DIM)

    def scores(task):
        h, i = task
        nk = i - max(0, i - (nb - 1)) + 1
        return _dot_nt(q_ref[rows(i), cols(h)], k_ref[keys(i), cols(h)]) + t_ref[h, :, (nb - nk) * blk:]

    def softmax_weights(zs):
        ms = [jnp.max(z, axis=1, keepdims=True) for z in zs]
        es = [jnp.exp(z - m) for z, m in zip(zs, ms)]
        return [e.astype(BF16) for e in es], [jnp.sum(e, axis=1, keepdims=True) for e in es]

    def finish(group, es, denoms):
        outs = [_dot(e, v_ref[keys(i), cols(h)]) for e, (h, i) in zip(es, group)]
        for (h, i), out, denom in zip(group, outs, denoms):
            o_ref[rows(i), cols(h)] = (out / denom).astype(o_ref.dtype)

    tasks = [(h, i) for h in range(CA_HEADS_PER_STEP) for i in range(q_ref.shape[0] // blk)]
    groups = [tasks[g:g + CA_GROUP] for g in range(0, len(tasks), CA_GROUP)]
    zs = [scores(task) for task in groups[0]]
    pending = None
    for n, group in enumerate(groups):
        next_zs = [scores(task) for task in groups[n + 1]] if n + 1 < len(groups) else None
        weights = softmax_weights(zs)
        if pending is not None:
            finish(*pending)
        pending = (group, *weights)
        zs = next_zs
    finish(*pending)


def _ca_attention(proj3d, table):
    b, s, _ = proj3d.shape
    width = CA_HEADS_PER_STEP * HEAD_DIM
    groups = W_ATT // width

    def seq_spec(first_group):
        return pl.BlockSpec((None, s, width), lambda bi, g: (bi, 0, first_group + g))

    return pl.pallas_call(
        _ca_kernel,
        grid=(b, groups),
        in_specs=[
            seq_spec(3 * groups),
            seq_spec(4 * groups),
            seq_spec(5 * groups),
            pl.BlockSpec((CA_HEADS_PER_STEP,) + table.shape[1:], lambda bi, g: (g, 0, 0)),
        ],
        out_specs=seq_spec(0),
        out_shape=jax.ShapeDtypeStruct((b, s, W_ATT), BF16),
        compiler_params=_params("parallel", "parallel"),
        name="ca_attention",
    )(proj3d, proj3d, proj3d, table)


def _resident(shape):
    return pl.BlockSpec(shape, lambda *_: (0,) * len(shape), pipeline_mode=pl.Buffered(1))


def _mix_kernel(ysb_ref, yca_ref, gsb_ref, gca_ref, x_ref, wsb_ref, wca_ref, wmix_ref, g_ref,
                x1_ref, h_ref):
    sb = _dot(ysb_ref[...], wsb_ref[...])
    ca = _dot(yca_ref[...], wca_ref[...])
    merged = (jax.nn.sigmoid(gsb_ref[...].astype(F32)) * sb
              + jax.nn.sigmoid(gca_ref[...].astype(F32)) * ca)
    x1 = x_ref[...] + _dot(merged.astype(BF16), wmix_ref[...])
    x1_ref[...] = x1
    h_ref[...] = _rmsnorm_f32(x1, g_ref[...]).astype(BF16)


def _mix_out(y_sb, y_ca, proj2d, x2d, w_sb, w_ca, w_mix, g_ffn):
    t = x2d.shape[0]
    tm = TM_MIX
    gate_blk = 6 * W_ATT // D_MODEL
    att_block = pl.BlockSpec((tm, W_ATT), lambda i: (i, 0))
    row_block = pl.BlockSpec((tm, D_MODEL), lambda i: (i, 0))
    return pl.pallas_call(
        _mix_kernel,
        grid=(t // tm,),
        in_specs=[
            att_block,
            att_block,
            pl.BlockSpec((tm, D_MODEL), lambda i: (i, gate_blk)),
            pl.BlockSpec((tm, D_MODEL), lambda i: (i, gate_blk + 1)),
            row_block,
            _resident((W_ATT, D_MODEL)),
            _resident((W_ATT, D_MODEL)),
            _resident((D_MODEL, D_MODEL)),
            _resident((1, D_MODEL)),
        ],
        out_specs=[row_block, row_block],
        out_shape=[jax.ShapeDtypeStruct((t, D_MODEL), F32), jax.ShapeDtypeStruct((t, D_MODEL), BF16)],
        compiler_params=_params("parallel"),
        name="mix_out",
    )(y_sb, y_ca, proj2d, proj2d, x2d, w_sb, w_ca, w_mix, g_ffn)


def _ffn_kernel(h_ref, wg_ref, wu_ref, wo_ref, o_ref):
    @pl.when(pl.program_id(1) == 0)
    def _():
        o_ref[...] = jnp.zeros_like(o_ref)

    h = h_ref[...]
    acts = []
    for p in range(0, wg_ref.shape[1], FFN_PIECE):
        gate = _dot(h, wg_ref[:, p:p + FFN_PIECE])
        up = _dot(h, wu_ref[:, p:p + FFN_PIECE])
        acts.append((gate * jax.nn.sigmoid(gate) * up).astype(BF16))
    o_ref[...] += _dot(jnp.concatenate(acts, axis=1), wo_ref[...])


def _ffn(h, w_in, w_out):
    t = h.shape[0]
    tm, tf = TM_FFN, TF_FFN
    nf = D_FF // tf
    return pl.pallas_call(
        _ffn_kernel,
        grid=(t // tm, nf),
        in_specs=[
            pl.BlockSpec((tm, D_MODEL), lambda i, f: (i, 0)),
            pl.BlockSpec((D_MODEL, tf), lambda i, f: (0, f)),
            pl.BlockSpec((D_MODEL, tf), lambda i, f: (0, nf + f)),
            pl.BlockSpec((tf, D_MODEL), lambda i, f: (f, 0)),
        ],
        out_specs=pl.BlockSpec((tm, D_MODEL), lambda i, f: (i, 0)),
        out_shape=jax.ShapeDtypeStruct((t, D_MODEL), F32),
        compiler_params=_params("parallel", "arbitrary"),
        name="ffn",
    )(h, w_in, w_in, w_out)


def _ple_kernel(x1_ref, y_ref, p_ref, gp_ref, gf_ref, wgate_ref, wple_ref, o_ref):
    x = x1_ref[...] + y_ref[...]
    h = _rmsnorm_f32(x, gp_ref[...]).astype(BF16)
    gate = jax.nn.sigmoid(_dot(h, wgate_ref[...]))
    emb = _dot(p_ref[...].astype(BF16), wple_ref[...])
    o_ref[...] = _rmsnorm_f32(x + gate * emb, gf_ref[...])


def _ple_final(x1, y_ffn, p_stacked, g_ple, g_final, w_gate, w_ple):
    t = x1.shape[0]
    tm = TM_PLE
    blocks_per_seq = p_stacked.shape[2] // tm
    row_block = pl.BlockSpec((tm, D_MODEL), lambda i: (i, 0))
    return pl.pallas_call(
        _ple_kernel,
        grid=(t // tm,),
        in_specs=[
            row_block,
            row_block,
            pl.BlockSpec((None, None, tm, D_PLE),
                         lambda i: (0, i // blocks_per_seq, i % blocks_per_seq, 0)),
            _resident((1, D_MODEL)),
            _resident((1, D_MODEL)),
            _resident((D_MODEL, D_MODEL)),
            _resident((D_PLE, D_MODEL)),
        ],
        out_specs=row_block,
        out_shape=jax.ShapeDtypeStruct((t, D_MODEL), F32),
        compiler_params=_params("parallel"),
        name="ple_final",
    )(x1, y_ffn, p_stacked, g_ple, g_final, w_gate, w_ple)


def kernel(x, p, w_in, w_sb_out, w_ca_out, w_mix_out, rel_bias, g_mix, g_ffn, g_ple, g_final,
           w_ffn_in, w_ffn_out, w_ple_in, w_ple_gate):
    b, s, d = x.shape
    assert w_in.shape[0] == 1, "the output norm is fused into the (single) layer's last kernel"
    xt = x.reshape(b * s, d)
    proj = _in_proj(xt, g_mix[0][None], w_in[0].astype(BF16))
    proj3d = proj.reshape(b, s, IN_COLS)
    y_sb, (w_sb, w_ca, w_mix, w_ffn_a, w_ffn_b, w_gate, w_ple) = _sb_attention(
        proj3d, [w_sb_out, w_ca_out, w_mix_out, w_ffn_in, w_ffn_out, w_ple_gate, w_ple_in])
    y_sb = y_sb.reshape(b * s, W_ATT)
    y_ca = _ca_attention(proj3d, _ca_bias_table(rel_bias[0])).reshape(b * s, W_ATT)
    x1, h_ffn = _mix_out(y_sb, y_ca, proj, xt, w_sb, w_ca, w_mix, g_ffn[0][None])
    y_ffn = _ffn(h_ffn, w_ffn_a, w_ffn_b)
    out = _ple_final(x1, y_ffn, p, g_ple[0][None], g_final[None],
                     w_gate, w_ple)
    return out.reshape(b, s, d)
```

```python
import functools

import jax
import jax.numpy as jnp
from jax import lax
from jax.experimental import pallas as pl
from jax.experimental.pallas import tpu as pltpu

D_MODEL = 2048
CHUNK = 64
HEAD_DIM = 128
N_HEADS = 8
W_ATT = N_HEADS * HEAD_DIM
LEFT_CHUNKS = 8
REL_CLIP = 128
N_REL = REL_CLIP + CHUNK
D_FF = 5632
D_PLE = 256
EPS = 1e-6
NEG = -1e30
IN_COLS = 6 * W_ATT + 2 * D_MODEL

BF16 = jnp.bfloat16
F32 = jnp.float32

BF16_SUBLANES = 16

VMEM_LIMIT_BYTES = 52 * 1024 * 1024

TM_INPROJ, TN_INPROJ = 1024, 2048
TM_MIX = 256
TM_FFN, TF_FFN, FFN_PIECE = 1024, 512, 256
TM_PLE = 512
SB_Q_BLOCK, SB_K_BLOCK = 512, 256
SB_DEAD_LOG_WEIGHT = -110.0
SB_HEADS_PER_STEP = 4
CA_BLOCK = 128
CA_KEY_BLOCKS = LEFT_CHUNKS * CHUNK // CA_BLOCK + 1
CA_GROUP = 2
CA_HEADS_PER_STEP = 4


def _params(*semantics):
    return pltpu.CompilerParams(dimension_semantics=semantics, vmem_limit_bytes=VMEM_LIMIT_BYTES)


def _rmsnorm_f32(x, g):
    ms = jnp.mean(x * x, axis=-1, keepdims=True)
    return x * lax.rsqrt(ms + EPS) * g


def _dot(a, b):
    return jnp.dot(a, b, preferred_element_type=F32)


def _dot_nt(a, b):
    return lax.dot_general(a, b, (((1,), (1,)), ((), ())), preferred_element_type=F32)


def _query_column_scale():
    col = jnp.arange(IN_COLS)[None, :]
    is_query = (col < W_ATT) | ((col >= 3 * W_ATT) & (col < 4 * W_ATT))
    return jnp.where(is_query, HEAD_DIM ** -0.5, 1.0).astype(F32)


def _inproj_kernel(x_ref, g_ref, w_ref, cs_ref, o_ref, h_ref):
    @pl.when(pl.program_id(1) == 0)
    def _():
        h_ref[...] = _rmsnorm_f32(x_ref[...], g_ref[...]).astype(BF16)

    o_ref[...] = (_dot(h_ref[...], w_ref[...]) * cs_ref[...]).astype(o_ref.dtype)


def _in_proj(x2d, g, w_bf16):
    t, d = x2d.shape
    n = w_bf16.shape[1]
    tm, tn = TM_INPROJ, TN_INPROJ
    return pl.pallas_call(
        _inproj_kernel,
        grid=(t // tm, n // tn),
        in_specs=[
            pl.BlockSpec((tm, d), lambda i, j: (i, 0)),
            pl.BlockSpec((1, d), lambda i, j: (0, 0)),
            pl.BlockSpec((d, tn), lambda i, j: (0, j)),
            pl.BlockSpec((1, tn), lambda i, j: (0, j)),
        ],
        out_specs=pl.BlockSpec((tm, tn), lambda i, j: (i, j)),
        out_shape=jax.ShapeDtypeStruct((t, n), BF16),
        scratch_shapes=[pltpu.VMEM((tm, d), BF16)],
        compiler_params=_params("parallel", "arbitrary"),
        name="in_proj",
    )(x2d, g, w_bf16, _query_column_scale())


def _sb_kernel(q_ref, k_ref, v_ref, *refs):
    *refs, acc_ref, carry_ref = refs
    n_cast = len(refs) // 2
    o_ref = refs[n_cast]
    for src_ref, dst_ref in zip(refs[:n_cast], refs[n_cast + 1:]):
        dst_ref[...] = src_ref[...].astype(dst_ref.dtype)

    tq, tk = SB_Q_BLOCK, SB_K_BLOCK
    assert tq == 2 * tk
    heads = range(SB_HEADS_PER_STEP)
    upper, lower, every = slice(0, tk), slice(tk, tq), slice(0, tq)
    i = pl.program_id(2)

    def cols(h):
        return slice(h * HEAD_DIM, (h + 1) * HEAD_DIM)

    row = lax.broadcasted_iota(jnp.int32, (tk, tk), 0)
    col = lax.broadcasted_iota(jnp.int32, (tk, tk), 1)
    suffix = jnp.where(row > col, 1.0, 0.0).astype(BF16)

    def gates(rows, j, diagonal):
        n_rows = rows.stop - rows.start
        start = pl.multiple_of(j * tk, tk)
        zs = [_dot_nt(q_ref[rows, cols(h)], k_ref[pl.ds(start, tk), cols(h)]) for h in heads]
        past = None
        if diagonal:
            past = (lax.broadcasted_iota(jnp.int32, (n_rows, tk), 1)
                    < lax.broadcasted_iota(jnp.int32, (n_rows, tk), 0))
        log_betas, log_keeps = [], []
        for z in zs:
            log_beta = jnp.minimum(z, 0.0) - jnp.log(1.0 + jnp.exp(-jnp.abs(z)))
            log_keep = log_beta - z
            if diagonal:
                log_keep = jnp.where(past, log_keep, 0.0)
            log_betas.append(log_beta)
            log_keeps.append(log_keep)
        sums = [_dot(lk.astype(BF16), suffix) for lk in log_keeps]
        return rows, start, past, log_betas, log_keeps, sums

    def accumulate(staged, first=False):
        rows, start, past, log_betas, log_keeps, sums = staged
        weights = []
        for h, log_beta, between in zip(heads, log_betas, sums):
            a = jnp.exp(log_beta + between if first else log_beta + between + carry_ref[h, rows])
            if past is not None:
                a = jnp.where(past, a, 0.0)
            weights.append(a.astype(BF16))
        for h, a, log_keep in zip(heads, weights, log_keeps):
            contribution = _dot(a, v_ref[pl.ds(start, tk), cols(h)])
            row_sum = jnp.sum(log_keep, axis=1, keepdims=True)
            if first:
                acc_ref[rows, cols(h)] = contribution
                carry_ref[h, rows] = row_sum
            else:
                acc_ref[rows, cols(h)] += contribution
                carry_ref[h, rows] += row_sum

    right = gates(lower, 2 * i + 1, True)
    left_upper = gates(upper, 2 * i, True)
    left_lower = gates(lower, 2 * i, False)
    accumulate(right, first=True)
    accumulate(left_upper, first=True)
    accumulate(left_lower)

    def sweep(rows, watched, j):
        def alive():
            return jnp.max(carry_ref[:, watched]) > SB_DEAD_LOG_WEIGHT

        def cond(state):
            j, live = state
            return jnp.logical_and(j >= 0, live)

        def body(state):
            j, _ = state
            accumulate(gates(rows, j, False))
            return j - 1, alive()

        return lax.while_loop(cond, body, (j, alive()))[0]

    j = sweep(every, lower, 2 * i - 1)
    sweep(upper, upper, j)
    o_ref[...] = acc_ref[...].astype(o_ref.dtype)


def _chunk_rows(rows, max_chunks):
    return next(r for r in range(BF16_SUBLANES, rows + 1, BF16_SUBLANES)
                if rows % r == 0 and rows // r <= max_chunks)


def _sb_attention(proj3d, stacked_weights):
    b, s, _ = proj3d.shape
    blk = SB_Q_BLOCK
    width = SB_HEADS_PER_STEP * HEAD_DIM
    groups = W_ATT // width
    q_blocks = s // blk
    n_steps = b * groups * q_blocks

    in_cast, out_cast, cast_shapes = [], [], []
    for w in stacked_weights:
        _, rows, cols = w.shape
        chunk = _chunk_rows(rows, n_steps)
        last = rows // chunk - 1

        def chunk_index(bi, g, i, last=last):
            return jnp.minimum((bi * groups + g) * q_blocks + i, last)

        in_cast.append(pl.BlockSpec((None, chunk, cols), lambda bi, g, i, f=chunk_index: (0, f(bi, g, i), 0)))
        out_cast.append(pl.BlockSpec((chunk, cols), lambda bi, g, i, f=chunk_index: (f(bi, g, i), 0)))
        cast_shapes.append(jax.ShapeDtypeStruct((rows, cols), BF16))

    y_spec = pl.BlockSpec((None, blk, width), lambda bi, g, i: (bi, i, g))
    y, *cast = pl.pallas_call(
        _sb_kernel,
        grid=(b, groups, q_blocks),
        in_specs=[
            y_spec,
            pl.BlockSpec((None, s, width), lambda bi, g, i: (bi, 0, groups + g)),
            pl.BlockSpec((None, s, width), lambda bi, g, i: (bi, 0, 2 * groups + g)),
        ] + in_cast,
        out_specs=[y_spec] + out_cast,
        out_shape=[jax.ShapeDtypeStruct((b, s, W_ATT), BF16)] + cast_shapes,
        scratch_shapes=[pltpu.VMEM((blk, width), F32), pltpu.VMEM((SB_HEADS_PER_STEP, blk, 1), F32)],
        compiler_params=_params("arbitrary", "arbitrary", "arbitrary"),
        name="sb_attention",
    )(proj3d, proj3d, proj3d, *stacked_weights)
    return y, cast


def _ca_bias_table(rel_bias):
    blk, width = CA_BLOCK, CA_KEY_BLOCKS * CA_BLOCK
    left = LEFT_CHUNKS * CHUNK
    h = rel_bias.shape[0]
    rb = rel_bias.astype(F32)
    n_lo = left + blk - 1 - REL_CLIP
    n_hi = blk - CHUNK + 1
    f = jnp.concatenate([jnp.broadcast_to(rb[:, :1], (h, n_lo)), rb,
                         jnp.broadcast_to(rb[:, -1:], (h, n_hi))], axis=1)
    n = f.shape[1]
    skew = jnp.tile(f, (1, blk))[:, :blk * (n - 1)].reshape(h, blk, n - 1)
    toeplitz = skew[:, :, blk - 1:blk - 1 + width]
    t = jnp.arange(blk)[:, None]
    c = jnp.arange(width)[None, :]
    dchunk = c // CHUNK - LEFT_CHUNKS - t // CHUNK
    valid = (dchunk >= -LEFT_CHUNKS) & (dchunk <= 0)
    return jnp.where(valid[None], toeplitz, NEG)


def _ca_kernel(q_ref, k_ref, v_ref, t_ref, o_ref):
    blk, nb = CA_BLOCK, CA_KEY_BLOCKS

    def rows(i):
        return slice(i * blk, (i + 1) * blk)

    def keys(i):
        return slice(max(0, i - (nb - 1)) * blk, (i + 1) * blk)

    def cols(h):
        return slice(h * HEAD_DIM, (h + 1) * HEAD_DIM)

    def scores(task):
        h, i = task
        nk = i - max(0, i - (nb - 1)) + 1
        return _dot_nt(q_ref[rows(i), cols(h)], k_ref[keys(i), cols(h)]) + t_ref[h, :, (nb - nk) * blk:]

    def softmax_weights(zs):
        ms = [jnp.max(z, axis=1, keepdims=True) for z in zs]
        return ([jnp.exp(z - m).astype(BF16) for z, m in zip(zs, ms)],)

    def finish(group, es):
        for e, (h, i) in zip(es, group):
            v = v_ref[keys(i), cols(h)]
            both = _dot(e, jnp.concatenate([v, jnp.ones_like(v)], axis=1))
            o_ref[rows(i), cols(h)] = (both[:, :HEAD_DIM] / both[:, HEAD_DIM:]).astype(o_ref.dtype)

    tasks = [(h, i) for h in range(CA_HEADS_PER_STEP) for i in range(q_ref.shape[0] // blk)]
    groups = [tasks[g:g + CA_GROUP] for g in range(0, len(tasks), CA_GROUP)]
    zs = [scores(task) for task in groups[0]]
    pending = None
    for n, group in enumerate(groups):
        next_zs = [scores(task) for task in groups[n + 1]] if n + 1 < len(groups) else None
        weights = softmax_weights(zs)
        if pending is not None:
            finish(*pending)
        pending = (group, *weights)
        zs = next_zs
    finish(*pending)


def _ca_attention(proj3d, table):
    b, s, _ = proj3d.shape
    width = CA_HEADS_PER_STEP * HEAD_DIM
    groups = W_ATT // width

    def seq_spec(first_group):
        return pl.BlockSpec((None, s, width), lambda bi, g: (bi, 0, first_group + g))

    return pl.pallas_call(
        _ca_kernel,
        grid=(b, groups),
        in_specs=[
            seq_spec(3 * groups),
            seq_spec(4 * groups),
            seq_spec(5 * groups),
            pl.BlockSpec((CA_HEADS_PER_STEP,) + table.shape[1:], lambda bi, g: (g, 0, 0)),
        ],
        out_specs=seq_spec(0),
        out_shape=jax.ShapeDtypeStruct((b, s, W_ATT), BF16),
        compiler_params=_params("parallel", "parallel"),
        name="ca_attention",
    )(proj3d, proj3d, proj3d, table)


def _resident(shape):
    return pl.BlockSpec(shape, lambda *_: (0,) * len(shape), pipeline_mode=pl.Buffered(1))


def _mix_kernel(ysb_ref, yca_ref, gsb_ref, gca_ref, x_ref, wsb_ref, wca_ref, wmix_ref, g_ref,
                x1_ref, h_ref):
    sb = _dot(ysb_ref[...], wsb_ref[...])
    ca = _dot(yca_ref[...], wca_ref[...])
    merged = (jax.nn.sigmoid(gsb_ref[...].astype(F32)) * sb
              + jax.nn.sigmoid(gca_ref[...].astype(F32)) * ca)
    x1 = x_ref[...] + _dot(merged.astype(BF16), wmix_ref[...])
    x1_ref[...] = x1
    h_ref[...] = _rmsnorm_f32(x1, g_ref[...]).astype(BF16)


def _mix_out(y_sb, y_ca, proj2d, x2d, w_sb, w_ca, w_mix, g_ffn):
    t = x2d.shape[0]
    tm = TM_MIX
    gate_blk = 6 * W_ATT // D_MODEL
    att_block = pl.BlockSpec((tm, W_ATT), lambda i: (i, 0))
    row_block = pl.BlockSpec((tm, D_MODEL), lambda i: (i, 0))
    return pl.pallas_call(
        _mix_kernel,
        grid=(t // tm,),
        in_specs=[
            att_block,
            att_block,
            pl.BlockSpec((tm, D_MODEL), lambda i: (i, gate_blk)),
            pl.BlockSpec((tm, D_MODEL), lambda i: (i, gate_blk + 1)),
            row_block,
            _resident((W_ATT, D_MODEL)),
            _resident((W_ATT, D_MODEL)),
            _resident((D_MODEL, D_MODEL)),
            _resident((1, D_MODEL)),
        ],
        out_specs=[row_block, row_block],
        out_shape=[jax.ShapeDtypeStruct((t, D_MODEL), F32), jax.ShapeDtypeStruct((t, D_MODEL), BF16)],
        compiler_params=_params("parallel"),
        name="mix_out",
    )(y_sb, y_ca, proj2d, proj2d, x2d, w_sb, w_ca, w_mix, g_ffn)


def _ffn_kernel(h_ref, wg_ref, wu_ref, wo_ref, o_ref):
    @pl.when(pl.program_id(1) == 0)
    def _():
        o_ref[...] = jnp.zeros_like(o_ref)

    h = h_ref[...]
    acts = []
    for p in range(0, wg_ref.shape[1], FFN_PIECE):
        gate = _dot(h, wg_ref[:, p:p + FFN_PIECE])
        up = _dot(h, wu_ref[:, p:p + FFN_PIECE])
        acts.append((gate * jax.nn.sigmoid(gate) * up).astype(BF16))
    o_ref[...] += _dot(jnp.concatenate(acts, axis=1), wo_ref[...])


def _ffn(h, w_in, w_out):
    t = h.shape[0]
    tm, tf = TM_FFN, TF_FFN
    nf = D_FF // tf
    return pl.pallas_call(
        _ffn_kernel,
        grid=(t // tm, nf),
        in_specs=[
            pl.BlockSpec((tm, D_MODEL), lambda i, f: (i, 0)),
            pl.BlockSpec((D_MODEL, tf), lambda i, f: (0, f)),
            pl.BlockSpec((D_MODEL, tf), lambda i, f: (0, nf + f)),
            pl.BlockSpec((tf, D_MODEL), lambda i, f: (f, 0)),
        ],
        out_specs=pl.BlockSpec((tm, D_MODEL), lambda i, f: (i, 0)),
        out_shape=jax.ShapeDtypeStruct((t, D_MODEL), F32),
        compiler_params=_params("parallel", "arbitrary"),
        name="ffn",
    )(h, w_in, w_in, w_out)


def _ple_kernel(x1_ref, y_ref, p_ref, gp_ref, gf_ref, wgate_ref, wple_ref, o_ref):
    x = x1_ref[...] + y_ref[...]
    h = _rmsnorm_f32(x, gp_ref[...]).astype(BF16)
    gate = jax.nn.sigmoid(_dot(h, wgate_ref[...]))
    emb = _dot(p_ref[...].astype(BF16), wple_ref[...])
    o_ref[...] = _rmsnorm_f32(x + gate * emb, gf_ref[...])


def _ple_final(x1, y_ffn, p_stacked, g_ple, g_final, w_gate, w_ple):
    t = x1.shape[0]
    tm = TM_PLE
    blocks_per_seq = p_stacked.shape[2] // tm
    row_block = pl.BlockSpec((tm, D_MODEL), lambda i: (i, 0))
    return pl.pallas_call(
        _ple_kernel,
        grid=(t // tm,),
        in_specs=[
            row_block,
            row_block,
            pl.BlockSpec((None, None, tm, D_PLE),
                         lambda i: (0, i // blocks_per_seq, i % blocks_per_seq, 0)),
            _resident((1, D_MODEL)),
            _resident((1, D_MODEL)),
            _resident((D_MODEL, D_MODEL)),
            _resident((D_PLE, D_MODEL)),
        ],
        out_specs=row_block,
        out_shape=jax.ShapeDtypeStruct((t, D_MODEL), F32),
        compiler_params=_params("parallel"),
        name="ple_final",
    )(x1, y_ffn, p_stacked, g_ple, g_final, w_gate, w_ple)


def kernel(x, p, w_in, w_sb_out, w_ca_out, w_mix_out, rel_bias, g_mix, g_ffn, g_ple, g_final,
           w_ffn_in, w_ffn_out, w_ple_in, w_ple_gate):
    b, s, d = x.shape
    assert w_in.shape[0] == 1, "the output norm is fused into the (single) layer's last kernel"
    xt = x.reshape(b * s, d)
    proj = _in_proj(xt, g_mix[0][None], w_in[0].astype(BF16))
    proj3d = proj.reshape(b, s, IN_COLS)
    y_sb, (w_sb, w_ca, w_mix, w_ffn_a, w_ffn_b, w_gate, w_ple) = _sb_attention(
        proj3d, [w_sb_out, w_ca_out, w_mix_out, w_ffn_in, w_ffn_out, w_ple_gate, w_ple_in])
    y_sb = y_sb.reshape(b * s, W_ATT)
    y_ca = _ca_attention(proj3d, _ca_bias_table(rel_bias[0])).reshape(b * s, W_ATT)
    x1, h_ffn = _mix_out(y_sb, y_ca, proj, xt, w_sb, w_ca, w_mix, g_ffn[0][None])
    y_ffn = _ffn(h_ffn, w_ffn_a, w_ffn_b)
    out = _ple_final(x1, y_ffn, p, g_ple[0][None], g_final[None],
                     w_gate, w_ple)
    return out.reshape(b, s, d)
```
